```python
import math
import jax, jax.numpy as jnp
from jax import lax
import numpy as np

D_MODEL = 1024
BATCH = 4
SEQ = 4096
DEPTH = 4
DEC_BATCH = 32
DEC_SEQ = 8
PAST_LEN = 8192
PAGE_SIZE = 128

N_EVEN = (DEPTH + 1) // 2
N_ODD = DEPTH // 2
ATT_HEADS = 8
HEAD_DIM = 64
ATT_WIDTH = ATT_HEADS * HEAD_DIM
MOBA_BLOCK = 256
MOBA_TOPK = 3
Q_BLOCK = 128
LRU_WIDTH = 512
LRU_BLOCKS = 8
LRU_BLOCK_DIM = LRU_WIDTH // LRU_BLOCKS
LRU_C = 8.0
CONV_W = 4
EVEN_IN = 3 * ATT_WIDTH + 2 * LRU_WIDTH
EVEN_MIX = ATT_WIDTH + LRU_WIDTH
SSD_INNER = 2 * D_MODEL
SSD_HEAD_DIM = 64
SSD_HEADS = SSD_INNER // SSD_HEAD_DIM
SSD_GROUPS = 8
SSD_HPG = SSD_HEADS // SSD_GROUPS
SSD_STATE = 128
SSD_CHUNK = 128
SSD_CONV_DIM = SSD_INNER + 2 * SSD_GROUPS * SSD_STATE
SSD_IN = SSD_INNER + SSD_CONV_DIM + SSD_HEADS
D_FF = 2816
N_EXPERTS = 8
TOP_K = 2
D_FF_EXPERT = 1408
DN_ALPHA = (2 * DEPTH) ** 0.25
DN_BETA = (8 * DEPTH) ** -0.25
LN_EPS = 1e-5
RMS_EPS = 1e-6

kernel_name = 'moba_rglru_ssd_hybrid_step'

F32 = jnp.float32


def layer_norm(x, g, b):
    xf = x.astype(F32)
    mu = jnp.mean(xf, -1, keepdims=True)
    var = jnp.mean(jnp.square(xf - mu), -1, keepdims=True)
    return ((xf - mu) * lax.rsqrt(var + LN_EPS) * g + b).astype(x.dtype)


def causal_conv(x, buf, w, b):
    L = x.shape[1]
    xp = jnp.concatenate([buf.astype(x.dtype), x], axis=1)
    y = b
    for t in range(CONV_W):
        y = y + xp[:, t:t + L] * w[t]
    return y.astype(x.dtype), xp[:, L:]


def rg_lru(u, h0, w_a, b_a, w_x, b_x, lam):
    Bn, L, _ = u.shape
    ub = u.astype(F32).reshape(Bn, L, LRU_BLOCKS, LRU_BLOCK_DIM)
    gate_r = jax.nn.sigmoid(jnp.einsum('blnd,nde->blne', ub, w_a.astype(F32)) + b_a)
    gate_i = jax.nn.sigmoid(jnp.einsum('blnd,nde->blne', ub, w_x.astype(F32)) + b_x)
    log_a = -LRU_C * gate_r * jax.nn.softplus(-lam.astype(F32))
    a = jnp.exp(log_a).reshape(Bn, L, LRU_WIDTH)
    bterm = (jnp.sqrt(-jnp.expm1(2.0 * log_a)) * (gate_i * ub)).reshape(Bn, L, LRU_WIDTH)
    bterm = bterm.at[:, 0].add(a[:, 0] * h0.astype(F32))

    def combine(left, right):
        return left[0] * right[0], right[0] * left[1] + right[1]

    _, h = lax.associative_scan(combine, (a, bterm), axis=1)
    return h, h[:, -1]


def moba_select(q, means, n_past, k_sel):
    s = jnp.einsum('bqhd,bnhd->bqhn', q.astype(F32), means)
    past = jnp.arange(means.shape[1])[None, :] < n_past[:, None]
    s = jnp.where(past[:, None, :], s, -jnp.inf)
    _, idx = lax.top_k(s, k_sel)
    valid = jnp.arange(k_sel)[None, :] < n_past[:, None]
    return idx, valid


def gather_blocks(blocks, idx):
    Bn, H = blocks.shape[0], blocks.shape[3]
    b_ix = jnp.arange(Bn)[:, None, None, None]
    h_ix = jnp.arange(H)[None, None, :, None]
    return blocks[b_ix, idx, :, h_ix, :]


def moba_core(q, k_own, v_own, own_mask, k_sel=None, v_sel=None, sel_valid=None):
    qf = q.astype(F32) * (HEAD_DIM ** -0.5)
    lo = jnp.einsum('bqhd,bkhd->bqhk', qf, k_own.astype(F32))
    lo = jnp.where(own_mask[:, None, :], lo, -jnp.inf)
    if k_sel is None:
        p = jax.nn.softmax(lo, axis=-1)
        out = jnp.einsum('bqhk,bkhd->bqhd', p, v_own.astype(F32))
    else:
        ls = jnp.einsum('bqhd,bqhskd->bqhsk', qf, k_sel.astype(F32))
        ls = jnp.where(sel_valid[:, None, :, None], ls, -jnp.inf)
        Bn, Q, H, S, K = ls.shape
        p = jax.nn.softmax(jnp.concatenate([ls.reshape(Bn, Q, H, S * K), lo], -1), axis=-1)
        ps = p[..., :S * K].reshape(Bn, Q, H, S, K)
        po = p[..., S * K:]
        out = (jnp.einsum('bqhsk,bqhskd->bqhd', ps, v_sel.astype(F32))
               + jnp.einsum('bqhk,bkhd->bqhd', po, v_own.astype(F32)))
    return out.astype(q.dtype)


def moba_prompt(q, k, v):
    Bn, S, H, D = q.shape
    nb = -(-S // MOBA_BLOCK)
    pad = nb * MOBA_BLOCK - S
    kp = jnp.pad(k, ((0, 0), (0, pad), (0, 0), (0, 0)))
    vp = jnp.pad(v, ((0, 0), (0, pad), (0, 0), (0, 0)))
    kb = kp.reshape(Bn, nb, MOBA_BLOCK, H, D)
    vb = vp.reshape(Bn, nb, MOBA_BLOCK, H, D)
    means = jnp.mean(kb.astype(F32), axis=2)
    k_sel = min(MOBA_TOPK, nb - 1)

    def one(i):
        q0 = i * Q_BLOCK
        j = q0 // MOBA_BLOCK
        qi = lax.dynamic_slice_in_dim(q, q0, Q_BLOCK, axis=1)
        ko = lax.dynamic_slice_in_dim(kp, j * MOBA_BLOCK, MOBA_BLOCK, axis=1)
        vo = lax.dynamic_slice_in_dim(vp, j * MOBA_BLOCK, MOBA_BLOCK, axis=1)
        qpos = q0 + jnp.arange(Q_BLOCK)
        kpos = j * MOBA_BLOCK + jnp.arange(MOBA_BLOCK)
        own_mask = kpos[None, :] <= qpos[:, None]
        if k_sel == 0:
            return moba_core(qi, ko, vo, own_mask)
        idx, valid = moba_select(qi, means, jnp.full((Q_BLOCK,), j), k_sel)
        return moba_core(qi, ko, vo, own_mask, gather_blocks(kb, idx), gather_blocks(vb, idx), valid)

    out = lax.map(one, jnp.arange(S // Q_BLOCK))
    return out.transpose(1, 0, 2, 3, 4).reshape(Bn, S, H * D)


def moba_sample(q, k, v, cache_k, cache_v, page_table):
    Bn, T, H, D = q.shape
    past = page_table.shape[1] * cache_k.shape[1]
    kp = cache_k[page_table].reshape(Bn, past, H, D)
    vp = cache_v[page_table].reshape(Bn, past, H, D)
    cur = past // MOBA_BLOCK
    start = cur * MOBA_BLOCK
    ko = jnp.concatenate([kp[:, start:].astype(k.dtype), k], axis=1)
    vo = jnp.concatenate([vp[:, start:].astype(v.dtype), v], axis=1)
    qpos = past + jnp.arange(T)
    kpos = start + jnp.arange(past - start + T)
    own_mask = kpos[None, :] <= qpos[:, None]
    k_sel = min(MOBA_TOPK, cur)
    if k_sel == 0:
        return moba_core(q, ko, vo, own_mask).reshape(Bn, T, H * D)
    kb = kp[:, :start].reshape(Bn, cur, MOBA_BLOCK, H, D)
    vb = vp[:, :start].reshape(Bn, cur, MOBA_BLOCK, H, D)
    means = jnp.mean(kb.astype(F32), axis=2)
    idx, valid = moba_select(q, means, jnp.full((T,), cur), k_sel)
    out = moba_core(q, ko, vo, own_mask, gather_blocks(kb, idx), gather_blocks(vb, idx), valid)
    return out.reshape(Bn, T, H * D)


def even_mixer(x, kv, conv_buf, h0, w_in, conv_w, conv_b, w_a, b_a, w_x, b_x, lam, w_out):
    Bn, L, _ = x.shape
    proj = x @ w_in
    q, k, v, u, gate = jnp.split(proj, [ATT_WIDTH, 2 * ATT_WIDTH, 3 * ATT_WIDTH, 3 * ATT_WIDTH + LRU_WIDTH], axis=-1)
    q = q.reshape(Bn, L, ATT_HEADS, HEAD_DIM)
    k = k.reshape(Bn, L, ATT_HEADS, HEAD_DIM)
    v = v.reshape(Bn, L, ATT_HEADS, HEAD_DIM)
    if kv is None:
        att = moba_prompt(q, k, v)
    else:
        att = moba_sample(q, k, v, kv[0], kv[1], kv[2])
    uc, conv_new = causal_conv(u, conv_buf, conv_w, conv_b)
    h, h_last = rg_lru(uc, h0, w_a, b_a, w_x, b_x, lam)
    rec = (h * jax.nn.gelu(gate.astype(F32))).astype(x.dtype)
    y = jnp.concatenate([att, rec], axis=-1) @ w_out
    return y, k, v, h_last.astype(h0.dtype), conv_new.astype(conv_buf.dtype)


def ssd_chunked(x, dt, A, bm, cm, h0, chunk):
    Bn, L, G, E, P = x.shape
    nc = L // chunk
    xf = x.astype(F32).reshape(Bn, nc, chunk, G, E, P)
    bf = bm.astype(F32).reshape(Bn, nc, chunk, G, SSD_STATE)
    cf = cm.astype(F32).reshape(Bn, nc, chunk, G, SSD_STATE)
    dtc = dt.reshape(Bn, nc, chunk, G, E)
    cs = jnp.cumsum(dtc * A, axis=2)
    cst = jnp.moveaxis(cs, 2, -1)
    causal = jnp.tril(jnp.ones((chunk, chunk), bool))
    decay_ls = jnp.exp(jnp.where(causal, cst[..., :, None] - cst[..., None, :], -jnp.inf))
    cb = jnp.einsum('bclgn,bcsgn->bcgls', cf, bf)
    m = cb[:, :, :, None] * decay_ls * jnp.moveaxis(dtc, 2, -1)[..., None, :]
    y = jnp.einsum('bcgels,bcsgep->bclgep', m, xf)
    decay_end = jnp.exp(cs[:, :, -1:] - cs)
    states = jnp.einsum('bclgn,bclge,bclgep->bcgepn', bf, decay_end * dtc, xf)
    chunk_decay = jnp.exp(cs[:, :, -1])

    def step(h, inp):
        dec, st = inp
        return dec[..., None, None] * h + st, h

    h_last, h_prev = lax.scan(step, h0, (jnp.moveaxis(chunk_decay, 1, 0), jnp.moveaxis(states, 1, 0)))
    h_prev = jnp.moveaxis(h_prev, 0, 1)
    y = y + jnp.einsum('bclgn,bcgepn,bclge->bclgep', cf, h_prev, jnp.exp(cs))
    return y.reshape(Bn, L, G, E, P), h_last


def ssd_mixer(x, conv_buf, h0, w_in, conv_w, conv_b, dt_bias, a_log, d_skip, norm_g, w_out):
    Bn, L, _ = x.shape
    proj = x @ w_in
    z, xbc, dt = jnp.split(proj, [SSD_INNER, SSD_INNER + SSD_CONV_DIM], axis=-1)
    xbc, conv_new = causal_conv(xbc, conv_buf, conv_w, conv_b)
    xbc = jax.nn.silu(xbc)
    xs, bm, cm = jnp.split(xbc, [SSD_INNER, SSD_INNER + SSD_GROUPS * SSD_STATE], axis=-1)
    xs = xs.reshape(Bn, L, SSD_GROUPS, SSD_HPG, SSD_HEAD_DIM)
    bm = bm.reshape(Bn, L, SSD_GROUPS, SSD_STATE)
    cm = cm.reshape(Bn, L, SSD_GROUPS, SSD_STATE)
    dt = jax.nn.softplus(dt.astype(F32) + dt_bias).reshape(Bn, L, SSD_GROUPS, SSD_HPG)
    A = -jnp.exp(a_log.astype(F32)).reshape(SSD_GROUPS, SSD_HPG)
    h0g = h0.astype(F32).reshape(Bn, SSD_GROUPS, SSD_HPG, SSD_HEAD_DIM, SSD_STATE)
    y, h_last = ssd_chunked(xs, dt, A, bm, cm, h0g, math.gcd(L, SSD_CHUNK))
    y = y + d_skip.astype(F32).reshape(SSD_GROUPS, SSD_HPG, 1) * xs.astype(F32)
    y = y.reshape(Bn, L, SSD_INNER) * jax.nn.silu(z.astype(F32))
    yg = y.reshape(Bn, L, SSD_GROUPS, SSD_INNER // SSD_GROUPS)
    yg = yg * lax.rsqrt(jnp.mean(jnp.square(yg), -1, keepdims=True) + RMS_EPS)
    y = (yg.reshape(Bn, L, SSD_INNER) * norm_g).astype(x.dtype)
    h_last = h_last.reshape(Bn, SSD_HEADS, SSD_HEAD_DIM, SSD_STATE).astype(h0.dtype)
    return y @ w_out, h_last, conv_new.astype(conv_buf.dtype)


def swiglu(x, w1, w3, w2):
    return (jax.nn.silu(x @ w1) * (x @ w3)) @ w2


def moe_swiglu(x, router, w1, w3, w2):
    logits = (x @ router).astype(F32)
    top_v, top_i = lax.top_k(logits, TOP_K)
    gates = jax.nn.softmax(top_v, axis=-1)
    out = jnp.zeros(x.shape, F32)
    for e in range(N_EXPERTS):
        g_e = jnp.sum(jnp.where(top_i == e, gates, 0.0), axis=-1)
        out = out + g_e[..., None] * swiglu(x, w1[e], w3[e], w2[e])
    return out.astype(x.dtype)


def run_trunk(x, kv, lru_conv0, lru_h0, ssd_conv0, ssd_h0, p):
    ks, vs, lru_h, lru_cv, ssm_h, ssd_cv = [], [], [], [], [], []
    for layer in range(DEPTH):
        i = layer // 2
        if layer % 2 == 0:
            kv_i = None if kv is None else (kv[0][i], kv[1][i], kv[2])
            mix, k, v, h, cv = even_mixer(x, kv_i, lru_conv0[i], lru_h0[i], p['w_in_even'][i], p['lru_conv_w'][i],
                                          p['lru_conv_b'][i], p['lru_w_a'][i], p['lru_b_a'][i], p['lru_w_x'][i],
                                          p['lru_b_x'][i], p['lru_lambda'][i], p['w_out_even'][i])
            ks.append(k)
            vs.append(v)
            lru_h.append(h)
            lru_cv.append(cv)
            x = layer_norm(DN_ALPHA * x + mix, p['ln_mix_g'][layer], p['ln_mix_b'][layer])
            ff = swiglu(x, p['ffn_w1'][i], p['ffn_w3'][i], p['ffn_w2'][i])
        else:
            mix, h, cv = ssd_mixer(x, ssd_conv0[i], ssd_h0[i], p['ssd_w_in'][i], p['ssd_conv_w'][i], p['ssd_conv_b'][i],
                                   p['ssd_dt_bias'][i], p['ssd_a_log'][i], p['ssd_d'][i], p['ssd_norm_g'][i],
                                   p['ssd_w_out'][i])
            ssm_h.append(h)
            ssd_cv.append(cv)
            x = layer_norm(DN_ALPHA * x + mix, p['ln_mix_g'][layer], p['ln_mix_b'][layer])
            ff = moe_swiglu(x, p['moe_router'][i], p['moe_w1'][i], p['moe_w3'][i], p['moe_w2'][i])
        x = layer_norm(DN_ALPHA * x + ff, p['ln_ffn_g'][layer], p['ln_ffn_b'][layer])
    return (x, jnp.stack(ks), jnp.stack(vs), jnp.stack(lru_h), jnp.stack(lru_cv),
            jnp.stack(ssm_h), jnp.stack(ssd_cv))


def setup_inputs(seed: int = 0) -> dict:
    key = jax.random.key(seed)
    keys = iter(jax.random.split(key, 64))

    def nrm(shape, scale):
        return jax.random.normal(next(keys), shape, F32) * scale

    def unif(shape, lo, hi):
        return jax.random.uniform(next(keys), shape, F32, lo, hi)

    n_pages = PAST_LEN // PAGE_SIZE
    n_pool = (DEC_BATCH * n_pages * 5) // 4
    perm = jax.random.permutation(next(keys), n_pool)
    page_table = perm[:DEC_BATCH * n_pages].reshape(DEC_BATCH, n_pages).astype(jnp.int32)
    a0 = unif((N_EVEN, LRU_BLOCKS, LRU_BLOCK_DIM), 0.9, 0.999)
    dt0 = jnp.exp(unif((N_ODD, SSD_HEADS), math.log(1e-3), math.log(1e-1)))
    return {
        'x_prompt': nrm((BATCH, SEQ, D_MODEL), 1.0),
        'x_sample': nrm((DEC_BATCH, DEC_SEQ, D_MODEL), 1.0),
        'cache_k': nrm((N_EVEN, n_pool, PAGE_SIZE, ATT_HEADS, HEAD_DIM), 1.0),
        'cache_v': nrm((N_EVEN, n_pool, PAGE_SIZE, ATT_HEADS, HEAD_DIM), 1.0),
        'page_table': page_table,
        'state_lru_h': nrm((N_EVEN, DEC_BATCH, LRU_WIDTH), 0.5),
        'state_lru_conv': nrm((N_EVEN, DEC_BATCH, CONV_W - 1, LRU_WIDTH), 1.0),
        'state_ssm': nrm((N_ODD, DEC_BATCH, SSD_HEADS, SSD_HEAD_DIM, SSD_STATE), 0.1),
        'state_ssd_conv': nrm((N_ODD, DEC_BATCH, CONV_W - 1, SSD_CONV_DIM), 1.0),
        'w_in_even': nrm((N_EVEN, D_MODEL, EVEN_IN), D_MODEL ** -0.5),
        'lru_conv_w': nrm((N_EVEN, CONV_W, LRU_WIDTH), CONV_W ** -0.5),
        'lru_conv_b': nrm((N_EVEN, LRU_WIDTH), 0.02),
        'lru_w_a': nrm((N_EVEN, LRU_BLOCKS, LRU_BLOCK_DIM, LRU_BLOCK_DIM), LRU_BLOCK_DIM ** -0.5),
        'lru_b_a': nrm((N_EVEN, LRU_BLOCKS, LRU_BLOCK_DIM), 0.02),
        'lru_w_x': nrm((N_EVEN, LRU_BLOCKS, LRU_BLOCK_DIM, LRU_BLOCK_DIM), LRU_BLOCK_DIM ** -0.5),
        'lru_b_x': nrm((N_EVEN, LRU_BLOCKS, LRU_BLOCK_DIM), 0.02),
        'lru_lambda': jnp.log(a0) - jnp.log1p(-a0),
        'w_out_even': nrm((N_EVEN, EVEN_MIX, D_MODEL), EVEN_MIX ** -0.5 * DN_BETA),
        'ssd_w_in': nrm((N_ODD, D_MODEL, SSD_IN), D_MODEL ** -0.5),
        'ssd_conv_w': nrm((N_ODD, CONV_W, SSD_CONV_DIM), CONV_W ** -0.5),
        'ssd_conv_b': nrm((N_ODD, SSD_CONV_DIM), 0.02),
        'ssd_dt_bias': dt0 + jnp.log(-jnp.expm1(-dt0)),
        'ssd_a_log': jnp.log(unif((N_ODD, SSD_HEADS), 1.0, 16.0)),
        'ssd_d': 1.0 + nrm((N_ODD, SSD_HEADS), 0.02),
        'ssd_norm_g': 1.0 + nrm((N_ODD, SSD_INNER), 0.02),
        'ssd_w_out': nrm((N_ODD, SSD_INNER, D_MODEL), SSD_INNER ** -0.5 * DN_BETA),
        'ffn_w1': nrm((N_EVEN, D_MODEL, D_FF), D_MODEL ** -0.5),
        'ffn_w3': nrm((N_EVEN, D_MODEL, D_FF), D_MODEL ** -0.5),
        'ffn_w2': nrm((N_EVEN, D_FF, D_MODEL), D_FF ** -0.5 * DN_BETA),
        'moe_router': nrm((N_ODD, D_MODEL, N_EXPERTS), D_MODEL ** -0.5),
        'moe_w1': nrm((N_ODD, N_EXPERTS, D_MODEL, D_FF_EXPERT), D_MODEL ** -0.5),
        'moe_w3': nrm((N_ODD, N_EXPERTS, D_MODEL, D_FF_EXPERT), D_MODEL ** -0.5),
        'moe_w2': nrm((N_ODD, N_EXPERTS, D_FF_EXPERT, D_MODEL), D_FF_EXPERT ** -0.5 * DN_BETA),
        'ln_mix_g': 1.0 + nrm((DEPTH, D_MODEL), 0.02),
        'ln_mix_b': nrm((DEPTH, D_MODEL), 0.02),
        'ln_ffn_g': 1.0 + nrm((DEPTH, D_MODEL), 0.02),
        'ln_ffn_b': nrm((DEPTH, D_MODEL), 0.02),
    }


def reference(x_prompt, x_sample, cache_k, cache_v, page_table, state_lru_h, state_lru_conv, state_ssm,
              state_ssd_conv, w_in_even, lru_conv_w, lru_conv_b, lru_w_a, lru_b_a, lru_w_x, lru_b_x, lru_lambda,
              w_out_even, ssd_w_in, ssd_conv_w, ssd_conv_b, ssd_dt_bias, ssd_a_log, ssd_d, ssd_norm_g, ssd_w_out,
              ffn_w1, ffn_w3, ffn_w2, moe_router, moe_w1, moe_w3, moe_w2, ln_mix_g, ln_mix_b, ln_ffn_g, ln_ffn_b):
    p = dict(w_in_even=w_in_even, lru_conv_w=lru_conv_w, lru_conv_b=lru_conv_b, lru_w_a=lru_w_a,
             lru_b_a=lru_b_a, lru_w_x=lru_w_x, lru_b_x=lru_b_x, lru_lambda=lru_lambda, w_out_even=w_out_even,
             ssd_w_in=ssd_w_in, ssd_conv_w=ssd_conv_w, ssd_conv_b=ssd_conv_b, ssd_dt_bias=ssd_dt_bias,
             ssd_a_log=ssd_a_log, ssd_d=ssd_d, ssd_norm_g=ssd_norm_g, ssd_w_out=ssd_w_out,
             ffn_w1=ffn_w1, ffn_w3=ffn_w3, ffn_w2=ffn_w2, moe_router=moe_router, moe_w1=moe_w1,
             moe_w3=moe_w3, moe_w2=moe_w2, ln_mix_g=ln_mix_g, ln_mix_b=ln_mix_b, ln_ffn_g=ln_ffn_g,
             ln_ffn_b=ln_ffn_b)
    bp = x_prompt.shape[0]
    dtp = x_prompt.dtype
    lru_conv0 = jnp.zeros((N_EVEN, bp, CONV_W - 1, LRU_WIDTH), dtp)
    lru_h0 = jnp.zeros((N_EVEN, bp, LRU_WIDTH), dtp)
    ssd_conv0 = jnp.zeros((N_ODD, bp, CONV_W - 1, SSD_CONV_DIM), dtp)
    ssd_h0 = jnp.zeros((N_ODD, bp, SSD_HEADS, SSD_HEAD_DIM, SSD_STATE), dtp)
    y_prompt, k_prompt, v_prompt, lru_h_prompt, lru_conv_prompt, ssm_prompt, ssd_conv_prompt = run_trunk(
        x_prompt, None, lru_conv0, lru_h0, ssd_conv0, ssd_h0, p)
    y_sample, k_sample, v_sample, lru_h_sample, lru_conv_sample, ssm_sample, ssd_conv_sample = run_trunk(
        x_sample, (cache_k, cache_v, page_table), state_lru_conv, state_lru_h, state_ssd_conv, state_ssm, p)
    return (y_prompt, y_sample, k_prompt, v_prompt, k_sample, v_sample, lru_h_prompt, lru_h_sample,
            lru_conv_prompt, lru_conv_sample, ssm_prompt, ssm_sample, ssd_conv_prompt, ssd_conv_sample)
```

```python
import functools
import math

import jax
import jax.numpy as jnp
from jax import lax
from jax.experimental import pallas as pl
from jax.experimental.pallas import tpu as pltpu

F32 = jnp.float32
BF16 = jnp.bfloat16

D_MODEL = 1024
DEPTH = 4
ATT_HEADS = 8
HEAD_DIM = 64
ATT_WIDTH = ATT_HEADS * HEAD_DIM
MOBA_BLOCK = 256
MOBA_TOPK = 3
LRU_WIDTH = 512
LRU_BLOCKS = 8
LRU_C = 8.0
CONV_W = 4
SSD_INNER = 2048
SSD_HEAD_DIM = 64
SSD_HEADS = 32
SSD_GROUPS = 8
SSD_HPG = 4
SSD_STATE = 128
SSD_CHUNK = 128
SSD_GROUP_W = SSD_HPG * SSD_HEAD_DIM
SSD_CONV_DIM = SSD_INNER + 2 * SSD_GROUPS * SSD_STATE
N_EXPERTS = 8
DN_ALPHA = (2 * DEPTH) ** 0.25
LN_EPS = 1e-5
RMS_EPS = 1e-6

LANES = 128
SUBLANES = 8
VMEM_LIMIT = 56 * 1024 * 1024
NEG = -1e30
HIGHEST = lax.Precision.HIGHEST


def _params(*sem):
    return pltpu.CompilerParams(dimension_semantics=sem, vmem_limit_bytes=VMEM_LIMIT)


def _dot(a, b):
    return jnp.dot(a, b, preferred_element_type=F32)


def _dot_nt(a, b, precision=None):
    return lax.dot_general(a, b, (((1,), (1,)), ((), ())), preferred_element_type=F32, precision=precision)


def _dot_tn(a, b):
    return lax.dot_general(a, b, (((0,), (0,)), ((), ())), preferred_element_type=F32)


def _sigmoid(x):
    return 1.0 / (1.0 + jnp.exp(-x))


def _silu(x):
    return x * _sigmoid(x)


def _softplus(x):
    return jnp.maximum(x, 0.0) + jnp.log1p(jnp.exp(-jnp.abs(x)))


def _gelu_tanh(x):
    return 0.5 * x * (1.0 + jnp.tanh(math.sqrt(2.0 / math.pi) * (x + 0.044715 * (x * x * x))))


def _layer_norm(y, g, b):
    mu = jnp.mean(y, axis=-1, keepdims=True)
    d = y - mu
    var = jnp.mean(d * d, axis=-1, keepdims=True)
    return d * lax.rsqrt(var + LN_EPS) * g + b


def _mm_kernel(x_ref, w_ref, o_ref):
    o_ref[...] = _dot(x_ref[...].astype(BF16), w_ref[...])


def matmul(x, w, n_cols, tm, tn):
    m, k = x.shape
    return pl.pallas_call(
        _mm_kernel,
        grid=(n_cols // tn, m // tm),
        in_specs=[pl.BlockSpec((tm, k), lambda j, i: (i, 0)),
                  pl.BlockSpec((k, tn), lambda j, i: (0, j))],
        out_specs=pl.BlockSpec((tm, tn), lambda j, i: (i, j)),
        out_shape=jax.ShapeDtypeStruct((m, n_cols), F32),
        compiler_params=_params("arbitrary", "arbitrary"),
        name="matmul",
    )(x, w)


def _mm_t_kernel(w_ref, x_ref, o_ref):
    o_ref[...] = _dot_nt(w_ref[...], x_ref[...].astype(BF16))


def matmul_t(w_t, x, n_batch, seq, ts):
    n, k = w_t.shape
    nblk = seq // ts
    return pl.pallas_call(
        _mm_t_kernel,
        grid=(n_batch, nblk),
        in_specs=[pl.BlockSpec((n, k), lambda b, s: (0, 0)),
                  pl.BlockSpec((ts, k), lambda b, s: (b * nblk + s, 0))],
        out_specs=pl.BlockSpec((None, n, ts), lambda b, s: (b, 0, s)),
        out_shape=jax.ShapeDtypeStruct((n_batch, n, seq), F32),
        compiler_params=_params("arbitrary", "arbitrary"),
        name="matmul_t",
    )(w_t, x)


def _mm_ln_kernel(*refs, n_pairs):
    a_refs = refs[:n_pairs]
    w_refs = refs[n_pairs:2 * n_pairs]
    r_ref, g_ref, b_ref, o_ref = refs[2 * n_pairs:]
    mix = _dot(a_refs[0][...].astype(BF16), w_refs[0][...])
    for a_ref, w_ref in zip(a_refs[1:], w_refs[1:]):
        mix = mix + _dot(a_ref[...].astype(BF16), w_ref[...])
    o_ref[...] = _layer_norm(DN_ALPHA * r_ref[...] + mix, g_ref[...], b_ref[...])


def matmul_residual_ln(acts, weights, resid, g, b, tm):
    m, d = resid.shape
    n_pairs = len(acts)
    in_specs = [pl.BlockSpec((tm, a.shape[1]), lambda i: (i, 0)) for a in acts]
    in_specs += [pl.BlockSpec(w.shape, lambda i: (0, 0)) for w in weights]
    in_specs += [pl.BlockSpec((tm, d), lambda i: (i, 0)),
                 pl.BlockSpec((1, d), lambda i: (0, 0)),
                 pl.BlockSpec((1, d), lambda i: (0, 0))]
    return pl.pallas_call(
        functools.partial(_mm_ln_kernel, n_pairs=n_pairs),
        grid=(m // tm,),
        in_specs=in_specs,
        out_specs=pl.BlockSpec((tm, d), lambda i: (i, 0)),
        out_shape=jax.ShapeDtypeStruct((m, d), F32),
        compiler_params=_params("arbitrary"),
        name="matmul_residual_ln",
    )(*acts, *weights, resid, g.reshape(1, d), b.reshape(1, d))


def _ffn_kernel(x_ref, gate_ref, w1_ref, w3_ref, w2_ref, g_ref, b_ref, o_ref, xb_ref, acc_ref, *, n_e, gated):
    e = pl.program_id(1)

    @pl.when(e == 0)
    def _():
        xb_ref[...] = x_ref[...].astype(BF16)
        acc_ref[...] = jnp.zeros_like(acc_ref)

    xb = xb_ref[...]
    h = _silu(_dot(xb, w1_ref[...])) * _dot(xb, w3_ref[...])
    if gated:
        lane = lax.broadcasted_iota(jnp.int32, (1, LANES), 1)
        ge = jnp.sum(jnp.where(lane == e, gate_ref[...], 0.0), axis=1, keepdims=True)
        h = h * ge
    acc_ref[...] += _dot(h.astype(BF16), w2_ref[...])

    @pl.when(e == n_e - 1)
    def _():
        o_ref[...] = _layer_norm(DN_ALPHA * x_ref[...] + acc_ref[...], g_ref[...], b_ref[...])


def swiglu_residual_ln(x, gates, w1, w3, w2, g, b, tm, n_e, gated):
    m, d = x.shape
    if gated:
        f = w1.shape[2]
        w13_spec = pl.BlockSpec((None, d, f), lambda i, e: (e, 0, 0))
        w2_spec = pl.BlockSpec((None, f, d), lambda i, e: (e, 0, 0))
    else:
        f = w1.shape[1] // n_e
        w13_spec = pl.BlockSpec((d, f), lambda i, e: (0, e))
        w2_spec = pl.BlockSpec((f, d), lambda i, e: (e, 0))
    return pl.pallas_call(
        functools.partial(_ffn_kernel, n_e=n_e, gated=gated),
        grid=(m // tm, n_e),
        in_specs=[pl.BlockSpec((tm, d), lambda i, e: (i, 0)),
                  pl.BlockSpec((tm, LANES), lambda i, e: (i, 0)),
                  w13_spec, w13_spec, w2_spec,
                  pl.BlockSpec((1, d), lambda i, e: (0, 0)),
                  pl.BlockSpec((1, d), lambda i, e: (0, 0))],
        out_specs=pl.BlockSpec((tm, d), lambda i, e: (i, 0)),
        out_shape=jax.ShapeDtypeStruct((m, d), F32),
        scratch_shapes=[pltpu.VMEM((tm, d), BF16), pltpu.VMEM((tm, d), F32)],
        compiler_params=_params("arbitrary", "arbitrary"),
        name="swiglu_residual_ln",
    )(x, gates, w1, w3, w2, g.reshape(1, d), b.reshape(1, d))


def _router_kernel(x_ref, rt_ref, o_ref):
    logits = _dot_nt(x_ref[...], rt_ref[...], precision=HIGHEST)
    lane = lax.broadcasted_iota(jnp.int32, logits.shape, 1)
    logits = jnp.where(lane < N_EXPERTS, logits, -jnp.inf)
    m1 = jnp.max(logits, axis=1, keepdims=True)
    i1 = jnp.min(jnp.where(logits == m1, lane, LANES), axis=1, keepdims=True)
    rest = jnp.where(lane == i1, -jnp.inf, logits)
    m2 = jnp.max(rest, axis=1, keepdims=True)
    i2 = jnp.min(jnp.where(rest == m2, lane, LANES), axis=1, keepdims=True)
    e2 = jnp.exp(m2 - m1)
    g1 = 1.0 / (1.0 + e2)
    g2 = e2 * g1
    o_ref[...] = jnp.where(lane == i1, g1, 0.0) + jnp.where(lane == i2, g2, 0.0)


def router_gates(x, router_t, tm):
    m, d = x.shape
    return pl.pallas_call(
        _router_kernel,
        grid=(m // tm,),
        in_specs=[pl.BlockSpec((tm, d), lambda i: (i, 0)),
                  pl.BlockSpec((LANES, d), lambda i: (0, 0))],
        out_specs=pl.BlockSpec((tm, LANES), lambda i: (i, 0)),
        out_shape=jax.ShapeDtypeStruct((m, LANES), F32),
        compiler_params=_params("arbitrary"),
        name="router_gates",
    )(x, router_t)


def _top_blocks(scores, n_valid):
    lane = lax.broadcasted_iota(jnp.int32, scores.shape, 1)
    s = jnp.where(lane < n_valid, scores, -jnp.inf)
    bias = jnp.full(scores.shape, NEG, F32)
    for _ in range(MOBA_TOPK):
        mx = jnp.max(s, axis=1, keepdims=True)
        hit = jnp.logical_and(s == mx, mx > -jnp.inf)
        idx = jnp.min(jnp.where(hit, lane, LANES), axis=1, keepdims=True)
        pick = lane == idx
        bias = jnp.where(pick, 0.0, bias)
        s = jnp.where(pick, -jnp.inf, s)
    return bias


def _moba_prompt_kernel(q_ref, kt_ref, vt_ref, kto_ref, vto_ref, o_ref,
                        ktb_ref, vtb_ref, means_ref, m_ref, l_ref, acc_ref, *, nb):
    qi = pl.program_id(2)
    tq = q_ref.shape[0]

    @pl.when(qi == 0)
    def _():
        kt = kt_ref[...]
        ktb_ref[...] = kt.astype(BF16)
        vtb_ref[...] = vt_ref[...].astype(BF16)
        lane = lax.broadcasted_iota(jnp.int32, (1, LANES), 1)
        means = jnp.zeros((LANES, LANES), F32)
        for n in range(nb):
            col = jnp.sum(kt[:, n * MOBA_BLOCK:(n + 1) * MOBA_BLOCK], axis=1, keepdims=True)
            means = jnp.where(lane == n, col * (1.0 / MOBA_BLOCK), means)
        means_ref[...] = means

    q = q_ref[...]
    lane = lax.broadcasted_iota(jnp.int32, (1, LANES), 1)
    row = lax.broadcasted_iota(jnp.int32, (tq, MOBA_BLOCK), 0)
    col = lax.broadcasted_iota(jnp.int32, (tq, MOBA_BLOCK), 1)
    causal = col <= row
    kto = kto_ref[...].astype(BF16)
    vto = vto_ref[...].astype(BF16)
    qbs, biases = [], []
    for h in range(2):
        head = jnp.logical_and(lane >= h * HEAD_DIM, lane < (h + 1) * HEAD_DIM)
        qh = jnp.where(head, q, 0.0)
        scores = _dot(qh, means_ref[...]) if False else lax.dot_general(
            qh, means_ref[...], (((1,), (0,)), ((), ())), preferred_element_type=F32, precision=HIGHEST)
        biases.append(_top_blocks(scores, qi))
        qb = (qh * (HEAD_DIM ** -0.5)).astype(BF16)
        qbs.append(qb)
        logits = jnp.where(causal, _dot(qb, kto), NEG)
        mx = jnp.max(logits, axis=1, keepdims=True)
        p = jnp.exp(logits - mx)
        m_ref[h] = jnp.broadcast_to(mx, (tq, LANES))
        l_ref[h] = jnp.broadcast_to(jnp.sum(p, axis=1, keepdims=True), (tq, LANES))
        acc_ref[h] = _dot_nt(p.astype(BF16), vto)

    for jb in range(nb - 1):
        @pl.when(jb < qi)
        def _(jb=jb):
            ktj = ktb_ref[:, jb * MOBA_BLOCK:(jb + 1) * MOBA_BLOCK]
            vtj = vtb_ref[:, jb * MOBA_BLOCK:(jb + 1) * MOBA_BLOCK]
            for h in range(2):
                logits = _dot(qbs[h], ktj) + biases[h][:, jb:jb + 1]
                m_prev = m_ref[h][:, :1]
                m_new = jnp.maximum(m_prev, jnp.max(logits, axis=1, keepdims=True))
                alpha = jnp.exp(m_prev - m_new)
                p = jnp.exp(logits - m_new)
                l_ref[h] = alpha * l_ref[h] + jnp.sum(p, axis=1, keepdims=True)
                acc_ref[h] = alpha * acc_ref[h] + _dot_nt(p.astype(BF16), vtj)
                m_ref[h] = jnp.broadcast_to(m_new, (tq, LANES))

    out0 = acc_ref[0] / l_ref[0][:, :1]
    out1 = acc_ref[1] / l_ref[1][:, :1]
    o_ref[...] = jnp.where(lane < HEAD_DIM, out0, out1)


def moba_prompt(proj, kvt, n_batch, seq):
    nb = seq // MOBA_BLOCK
    tq = MOBA_BLOCK
    n_pairs = ATT_WIDTH // LANES
    return pl.pallas_call(
        functools.partial(_moba_prompt_kernel, nb=nb),
        grid=(n_batch, n_pairs, nb),
        in_specs=[pl.BlockSpec((tq, LANES), lambda b, p, i: (b * nb + i, p)),
                  pl.BlockSpec((None, LANES, seq), lambda b, p, i: (b, p, 0)),
                  pl.BlockSpec((None, LANES, seq), lambda b, p, i: (b, n_pairs + p, 0)),
                  pl.BlockSpec((None, LANES, MOBA_BLOCK), lambda b, p, i: (b, p, i)),
                  pl.BlockSpec((None, LANES, MOBA_BLOCK), lambda b, p, i: (b, n_pairs + p, i))],
        out_specs=pl.BlockSpec((tq, LANES), lambda b, p, i: (b * nb + i, p)),
        out_shape=jax.ShapeDtypeStruct((n_batch * seq, ATT_WIDTH), F32),
        scratch_shapes=[pltpu.VMEM((LANES, seq), BF16), pltpu.VMEM((LANES, seq), BF16),
                        pltpu.VMEM((LANES, LANES), F32),
                        pltpu.VMEM((2, tq, LANES), F32), pltpu.VMEM((2, tq, LANES), F32),
                        pltpu.VMEM((2, tq, LANES), F32)],
        compiler_params=_params("arbitrary", "arbitrary", "arbitrary"),
        name="moba_prompt",
    )(proj, kvt, kvt, kvt, kvt)


def _moba_sample_kernel(pt_ref, q_ref, ktn_ref, vn_ref, k0_ref, k1_ref, v0_ref, v1_ref, o_ref,
                        qbd_ref, ksum_ref, m_ref, l_ref, oblk_ref, *, n_blocks, layer):
    del pt_ref, layer
    n = pl.program_id(1)
    rows = ATT_HEADS * q_ref.shape[0]
    t_new = q_ref.shape[0]
    lane = lax.broadcasted_iota(jnp.int32, (1, LANES), 1)

    @pl.when(n == 0)
    def _():
        q = q_ref[...]
        col_head = lax.broadcasted_iota(jnp.int32, (t_new, ATT_WIDTH), 1) // HEAD_DIM
        for h in range(ATT_HEADS):
            qbd_ref[h * t_new:(h + 1) * t_new, :] = jnp.where(col_head == h, q, 0.0)
        ksum_ref[...] = jnp.zeros_like(ksum_ref)
        m_ref[...] = jnp.zeros_like(m_ref)
        l_ref[...] = jnp.zeros_like(l_ref)

    kt = jnp.concatenate([k0_ref[...].reshape(ATT_WIDTH, LANES), k1_ref[...].reshape(ATT_WIDTH, LANES)], axis=1)
    vt = jnp.concatenate([v0_ref[...].reshape(ATT_WIDTH, LANES), v1_ref[...].reshape(ATT_WIDTH, LANES)], axis=1)
    qb = (qbd_ref[...] * (HEAD_DIM ** -0.5)).astype(BF16)
    logits = _dot(qb, kt.astype(BF16))
    mx = jnp.max(logits, axis=1, keepdims=True)
    p = jnp.exp(logits - mx)
    ls = jnp.sum(p, axis=1, keepdims=True)
    oblk_ref[n] = _dot_nt(p.astype(BF16), vt.astype(BF16))
    m_ref[...] = jnp.where(lane == n, mx, m_ref[...])
    l_ref[...] = jnp.where(lane == n, ls, l_ref[...])
    ksum_ref[...] = jnp.where(lane == n, jnp.sum(kt, axis=1, keepdims=True), ksum_ref[...])

    @pl.when(n == n_blocks - 1)
    def _():
        qbd = qbd_ref[...]
        scores = lax.dot_general(qbd, ksum_ref[...] * (1.0 / MOBA_BLOCK), (((1,), (0,)), ((), ())),
                                 preferred_element_type=F32, precision=HIGHEST)
        bias = _top_blocks(scores, n_blocks)
        qb2 = (qbd * (HEAD_DIM ** -0.5)).astype(BF16)
        lo = _dot(qb2, ktn_ref[...].astype(BF16))
        r_q = lax.broadcasted_iota(jnp.int32, (rows, t_new), 0) % t_new
        c_t = lax.broadcasted_iota(jnp.int32, (rows, t_new), 1)
        lo = jnp.where(c_t <= r_q, lo, NEG)
        m_own = jnp.max(lo, axis=1, keepdims=True)
        p_own = jnp.exp(lo - m_own)
        l_own = jnp.sum(p_own, axis=1, keepdims=True)
        o_own = _dot(p_own.astype(BF16), vn_ref[...].astype(BF16))
        m_sel = m_ref[...] + bias
        m_tot = jnp.maximum(jnp.max(m_sel, axis=1, keepdims=True), m_own)
        w = jnp.exp(m_sel - m_tot)
        w_own = jnp.exp(m_own - m_tot)
        l_tot = jnp.sum(w * l_ref[...], axis=1, keepdims=True) + w_own * l_own
        o_tot = w_own * o_own
        for j in range(n_blocks):
            o_tot = o_tot + w[:, j:j + 1] * oblk_ref[j]
        o_tot = o_tot / l_tot
        col_head = lax.broadcasted_iota(jnp.int32, (t_new, ATT_WIDTH), 1) // HEAD_DIM
        out = jnp.zeros((t_new, ATT_WIDTH), F32)
        for h in range(ATT_HEADS):
            out = jnp.where(col_head == h, o_tot[h * t_new:(h + 1) * t_new, :], out)
        o_ref[...] = out


def moba_sample(proj_s, ktn, cache_kt, cache_vt, page_table, layer, t_new):
    n_seq, n_pages = page_table.shape
    pages_per_block = MOBA_BLOCK // cache_kt.shape[-1]
    assert pages_per_block == 2
    n_blocks = n_pages // pages_per_block
    rows = ATT_HEADS * t_new
    page_shape = (None, None) + cache_kt.shape[2:]

    def page_spec(j):
        return pl.BlockSpec(page_shape, lambda b, n, pt: (layer, pt[b, 2 * n + j], 0, 0, 0))

    grid_spec = pltpu.PrefetchScalarGridSpec(
        num_scalar_prefetch=1,
        grid=(n_seq, n_blocks),
        in_specs=[pl.BlockSpec((t_new, ATT_WIDTH), lambda b, n, pt: (b, 0)),
                  pl.BlockSpec((None, ATT_WIDTH, t_new), lambda b, n, pt: (b, 0, 0)),
                  pl.BlockSpec((t_new, ATT_WIDTH), lambda b, n, pt: (b, 2)),
                  page_spec(0), page_spec(1), page_spec(0), page_spec(1)],
        out_specs=pl.BlockSpec((t_new, ATT_WIDTH), lambda b, n, pt: (b, 0)),
        scratch_shapes=[pltpu.VMEM((rows, ATT_WIDTH), F32),
                        pltpu.VMEM((ATT_WIDTH, LANES), F32),
                        pltpu.VMEM((rows, LANES), F32), pltpu.VMEM((rows, LANES), F32),
                        pltpu.VMEM((n_blocks, rows, ATT_WIDTH), F32)],
    )
    return pl.pallas_call(
        functools.partial(_moba_sample_kernel, n_blocks=n_blocks, layer=layer),
        grid_spec=grid_spec,
        out_shape=jax.ShapeDtypeStruct((n_seq * t_new, ATT_WIDTH), F32),
        compiler_params=_params("arbitrary", "arbitrary"),
        name="moba_sample",
    )(page_table, proj_s, ktn, proj_s, cache_kt, cache_kt, cache_vt, cache_vt)


def _shift_rows(x, d, fill):
    row = lax.broadcasted_iota(jnp.int32, x.shape, 0)
    return jnp.where(row >= d, pltpu.roll(x, d, 0), fill)


def _lru_kernel(u_ref, gate_ref, cbuf_ref, h0_ref, cw_ref, cb_ref, wa_ref, ba_ref, wx_ref, bx_ref, lam_ref,
                rec_ref, hl_ref, xbuf, hc, *, tc, n_chunks):
    c = pl.program_id(1)

    @pl.when(c == 0)
    def _():
        xbuf[0:SUBLANES, :] = cbuf_ref[...]
        hc[...] = h0_ref[...]

    xbuf[SUBLANES:SUBLANES + tc, :] = u_ref[...]
    w = cw_ref[...]
    uc = cb_ref[...]
    for t in range(CONV_W):
        off = SUBLANES - (CONV_W - 1) + t
        uc = uc + xbuf[off:off + tc, :] * w[t:t + 1, :]
    xbuf[0:SUBLANES, :] = xbuf[tc:tc + SUBLANES, :]

    ub = uc.astype(BF16)
    gate_r = _sigmoid(_dot(ub, wa_ref[...]) + ba_ref[...])
    gate_i = _sigmoid(_dot(ub, wx_ref[...]) + bx_ref[...])
    log_a = -LRU_C * gate_r * _softplus(-lam_ref[...])
    a = jnp.exp(log_a)
    bterm = jnp.sqrt(-jnp.tanh(log_a) * (a * a + 1.0)) * (gate_i * uc)
    row = lax.broadcasted_iota(jnp.int32, a.shape, 0)
    bterm = bterm + jnp.where(row == 0, a * hc[...], 0.0)
    d = 1
    while d < tc:
        b_sh = _shift_rows(bterm, d, 0.0)
        a_sh = _shift_rows(a, d, 1.0)
        bterm = a * b_sh + bterm
        a = a * a_sh
        d *= 2
    h = bterm
    hc[...] = h[tc - 1:tc, :]
    rec_ref[...] = h * _gelu_tanh(gate_ref[...])

    @pl.when(c == n_chunks - 1)
    def _():
        hl_ref[...] = h[tc - 1:tc, :]


def lru_mixer(proj, row0, n_batch, seq, tc, cbuf, h0, cw, cb, wa, ba, wx, bx, lam):
    n_chunks = seq // tc
    blk0 = row0 // tc
    vec = lambda: pl.BlockSpec((1, LRU_WIDTH), lambda b, c: (0, 0))
    return pl.pallas_call(
        functools.partial(_lru_kernel, tc=tc, n_chunks=n_chunks),
        grid=(n_batch, n_chunks),
        in_specs=[pl.BlockSpec((tc, LRU_WIDTH), lambda b, c: (blk0 + b * n_chunks + c, 3)),
                  pl.BlockSpec((tc, LRU_WIDTH), lambda b, c: (blk0 + b * n_chunks + c, 4)),
                  pl.BlockSpec((None, SUBLANES, LRU_WIDTH), lambda b, c: (b, 0, 0)),
                  pl.BlockSpec((None, 1, LRU_WIDTH), lambda b, c: (b, 0, 0)),
                  pl.BlockSpec((CONV_W, LRU_WIDTH), lambda b, c: (0, 0)),
                  vec(),
                  pl.BlockSpec((LRU_WIDTH, LRU_WIDTH), lambda b, c: (0, 0)),
                  vec(),
                  pl.BlockSpec((LRU_WIDTH, LRU_WIDTH), lambda b, c: (0, 0)),
                  vec(), vec()],
        out_specs=[pl.BlockSpec((tc, LRU_WIDTH), lambda b, c: (b * n_chunks + c, 0)),
                   pl.BlockSpec((None, 1, LRU_WIDTH), lambda b, c: (b, 0, 0))],
        out_shape=[jax.ShapeDtypeStruct((n_batch * seq, LRU_WIDTH), F32),
                   jax.ShapeDtypeStruct((n_batch, 1, LRU_WIDTH), F32)],
        scratch_shapes=[pltpu.VMEM((SUBLANES + tc, LRU_WIDTH), F32), pltpu.VMEM((1, LRU_WIDTH), F32)],
        compiler_params=_params("arbitrary", "arbitrary"),
        name="lru_mixer",
    )(proj, proj, cbuf, h0, cw, cb, wa, ba, wx, bx, lam)


def _expand_heads(v, g):
    lane = lax.broadcasted_iota(jnp.int32, (1, SSD_GROUP_W), 1)
    out = v[:, SSD_HPG * g + SSD_HPG - 1:SSD_HPG * g + SSD_HPG]
    for k in range(SSD_HPG - 2, -1, -1):
        out = jnp.where(lane < (k + 1) * SSD_HEAD_DIM, v[:, SSD_HPG * g + k:SSD_HPG * g + k + 1], out)
    return out


def _ssd_kernel(x_ref, zx_ref, cbuf_ref, h0_ref, wdt_ref, wdtt_ref, dtb_ref, dtbt_ref, alog_ref, alogt_ref,
                cw_ref, cb_ref, dskip_ref, ng_ref, y_ref, hl_ref, xbuf, st_ref, *, tc, n_chunks):
    c = pl.program_id(1)

    @pl.when(c == 0)
    def _():
        xbuf[0:SUBLANES, :] = cbuf_ref[...]
        for g in range(SSD_GROUPS):
            st_ref[g] = h0_ref[g * SSD_GROUP_W:(g + 1) * SSD_GROUP_W, :].T

    xbuf[SUBLANES:SUBLANES + tc, :] = zx_ref[:, SSD_INNER:]
    cw = cw_ref[...]
    cb = cb_ref[...]

    def conv_silu(lo, width):
        acc = cb[:, lo:lo + width]
        for t in range(CONV_W):
            off = SUBLANES - (CONV_W - 1) + t
            acc = acc + xbuf[off:off + tc, lo:lo + width] * cw[t:t + 1, lo:lo + width]
        return _silu(acc)

    xb = x_ref[...].astype(BF16)
    dt = _softplus(_dot(xb, wdt_ref[...]) + dtb_ref[...])
    dtt = _softplus(_dot_nt(wdtt_ref[...], xb) + dtbt_ref[...])
    a_neg = -jnp.exp(alog_ref[...])
    a_negt = -jnp.exp(alogt_ref[...])
    r = lax.broadcasted_iota(jnp.int32, (tc, tc), 0)
    s = lax.broadcasted_iota(jnp.int32, (tc, tc), 1)
    causal = s <= r
    tri = causal.astype(F32)
    cs = jnp.dot(tri, dt * a_neg, preferred_element_type=F32, precision=HIGHEST)
    cst = _dot_nt(dtt * a_negt, tri, precision=HIGHEST)
    cs_end = cs[tc - 1:tc, :]
    w_end = jnp.exp(cs_end - cs) * dt
    ecs = jnp.exp(cs)
    chunk_decay = jnp.exp(cs_end)
    lane = lax.broadcasted_iota(jnp.int32, (1, SSD_GROUP_W), 1)

    for g in range(SSD_GROUPS):
        xg = conv_silu(g * SSD_GROUP_W, SSD_GROUP_W)
        bg = conv_silu(SSD_INNER + g * SSD_STATE, SSD_STATE).astype(BF16)
        cg = conv_silu(SSD_INNER + SSD_GROUPS * SSD_STATE + g * SSD_STATE, SSD_STATE).astype(BF16)
        cbm = _dot_nt(cg, bg)
        y = jnp.zeros((tc, SSD_GROUP_W), F32)
        for k in range(SSD_HPG):
            e = SSD_HPG * g + k
            dec = jnp.exp(jnp.where(causal, cs[:, e:e + 1] - cst[e:e + 1, :], -jnp.inf))
            m = (cbm * dec * dtt[e:e + 1, :]).astype(BF16)
            head = jnp.logical_and(lane >= k * SSD_HEAD_DIM, lane < (k + 1) * SSD_HEAD_DIM)
            y = y + _dot(m, jnp.where(head, xg, 0.0).astype(BF16))
        st = st_ref[g]
        y = y + _dot(cg, st.astype(BF16)) * _expand_heads(ecs, g)
        xw = (xg * _expand_heads(w_end, g)).astype(BF16)
        st_ref[g] = st * _expand_heads(chunk_decay, g) + _dot_tn(bg, xw)
        y = y + dskip_ref[:, g * SSD_GROUP_W:(g + 1) * SSD_GROUP_W] * xg
        y = y * _silu(zx_ref[:, g * SSD_GROUP_W:(g + 1) * SSD_GROUP_W])
        y = y * lax.rsqrt(jnp.mean(y * y, axis=-1, keepdims=True) + RMS_EPS)
        y_ref[:, g * SSD_GROUP_W:(g + 1) * SSD_GROUP_W] = y * ng_ref[:, g * SSD_GROUP_W:(g + 1) * SSD_GROUP_W]

    xbuf[0:SUBLANES, :] = xbuf[tc:tc + SUBLANES, :]

    @pl.when(c == n_chunks - 1)
    def _():
        for g in range(SSD_GROUPS):
            hl_ref[g * SSD_GROUP_W:(g + 1) * SSD_GROUP_W, :] = st_ref[g].T


def ssd_mixer(x, zx, row0, n_batch, seq, tc, cbuf, h0, wdt, wdtt, dtb, alog, cw, cb, dskip, ng):
    n_chunks = seq // tc
    blk0 = row0 // tc
    m_state = SSD_HEADS * SSD_HEAD_DIM
    const2 = lambda shape: pl.BlockSpec(shape, lambda b, c: (0, 0))
    return pl.pallas_call(
        functools.partial(_ssd_kernel, tc=tc, n_chunks=n_chunks),
        grid=(n_batch, n_chunks),
        in_specs=[pl.BlockSpec((tc, D_MODEL), lambda b, c: (blk0 + b * n_chunks + c, 0)),
                  pl.BlockSpec((tc, SSD_INNER + SSD_CONV_DIM), lambda b, c: (blk0 + b * n_chunks + c, 0)),
                  pl.BlockSpec((None, SUBLANES, SSD_CONV_DIM), lambda b, c: (b, 0, 0)),
                  pl.BlockSpec((None, m_state, SSD_STATE), lambda b, c: (b, 0, 0)),
                  const2((D_MODEL, LANES)), const2((LANES, D_MODEL)),
                  const2((1, LANES)), const2((LANES, 1)), const2((1, LANES)), const2((LANES, 1)),
                  const2((CONV_W, SSD_CONV_DIM)), const2((1, SSD_CONV_DIM)),
                  const2((1, SSD_INNER)), const2((1, SSD_INNER))],
        out_specs=[pl.BlockSpec((tc, SSD_INNER), lambda b, c: (b * n_chunks + c, 0)),
                   pl.BlockSpec((None, m_state, SSD_STATE), lambda b, c: (b, 0, 0))],
        out_shape=[jax.ShapeDtypeStruct((n_batch * seq, SSD_INNER), F32),
                   jax.ShapeDtypeStruct((n_batch, m_state, SSD_STATE), F32)],
        scratch_shapes=[pltpu.VMEM((SUBLANES + tc, SSD_CONV_DIM), F32),
                        pltpu.VMEM((SSD_GROUPS, SSD_STATE, SSD_GROUP_W), F32)],
        compiler_params=_params("arbitrary", "arbitrary"),
        name="ssd_mixer",
    )(x, zx, cbuf, h0, wdt, wdtt, dtb.reshape(1, LANES), dtb.reshape(LANES, 1),
      alog.reshape(1, LANES), alog.reshape(LANES, 1), cw, cb, dskip, ng)


def _block_diag(w):
    n, d, e = w.shape
    eye = jnp.eye(n, dtype=w.dtype)
    return (eye[:, None, :, None] * w[:, :, None, :]).reshape(n * d, n * e)


def _conv_tail(buf):
    return jnp.pad(buf, ((0, 0), (SUBLANES - (CONV_W - 1), 0), (0, 0)))


def kernel(x_prompt, x_sample, cache_k, cache_v, page_table, state_lru_h, state_lru_conv, state_ssm, state_ssd_conv, w_in_even, lru_conv_w, lru_conv_b, lru_w_a, lru_b_a, lru_w_x, lru_b_x, lru_lambda, w_out_even, ssd_w_in, ssd_conv_w, ssd_conv_b, ssd_dt_bias, ssd_a_log, ssd_d, ssd_norm_g, ssd_w_out, ffn_w1, ffn_w3, ffn_w2, moe_router, moe_w1, moe_w3, moe_w2, ln_mix_g, ln_mix_b, ln_ffn_g, ln_ffn_b):
    bp, sp, d = x_prompt.shape
    bs, ts, _ = x_sample.shape
    mp = bp * sp
    ms = bs * ts
    m = mp + ms
    tm = 640 if m % 640 == 0 else ts * 8
    x = jnp.concatenate([x_prompt.reshape(mp, d), x_sample.reshape(ms, d)], axis=0)

    cache_kt = cache_k.transpose(0, 1, 3, 4, 2)
    cache_vt = cache_v.transpose(0, 1, 3, 4, 2)

    ks_p, vs_p, ks_s, vs_s = [], [], [], []
    lruh_p, lruh_s, lrucv_p, lrucv_s = [], [], [], []
    ssm_p, ssm_s, ssdcv_p, ssdcv_s = [], [], [], []
    zeros_lru_cbuf = jnp.zeros((bp, SUBLANES, LRU_WIDTH), F32)
    zeros_lru_h = jnp.zeros((bp, 1, LRU_WIDTH), F32)
    zeros_ssd_cbuf = jnp.zeros((bp, SUBLANES, SSD_CONV_DIM), F32)
    zeros_ssm = jnp.zeros((bp, SSD_HEADS * SSD_HEAD_DIM, SSD_STATE), F32)

    for layer in range(DEPTH):
        i = layer // 2
        if layer % 2 == 0:
            w_in = w_in_even[i]
            proj = matmul(x, w_in.astype(BF16), w_in.shape[1], tm, 1280)
            w_kvt = w_in[:, ATT_WIDTH:3 * ATT_WIDTH].T.astype(BF16)
            kvt = matmul_t(w_kvt, x, bp, sp, 512 if sp % 512 == 0 else sp)
            att_p = moba_prompt(proj, kvt, bp, sp)
            proj_s = proj[mp:]
            k_s = proj_s[:, ATT_WIDTH:2 * ATT_WIDTH]
            v_s = proj_s[:, 2 * ATT_WIDTH:3 * ATT_WIDTH]
            ktn = k_s.reshape(bs, ts, ATT_WIDTH).transpose(0, 2, 1)
            att_s = moba_sample(proj_s, ktn, cache_kt, cache_vt, page_table, i, ts)

            wa = _block_diag(lru_w_a[i]).astype(BF16)
            wx = _block_diag(lru_w_x[i]).astype(BF16)
            lru_args = (lru_conv_w[i], lru_conv_b[i].reshape(1, -1), wa, lru_b_a[i].reshape(1, -1),
                        wx, lru_b_x[i].reshape(1, -1), lru_lambda[i].reshape(1, -1))
            tc_p = 256 if sp % 256 == 0 else sp
            rec_p, hl_p = lru_mixer(proj, 0, bp, sp, tc_p, zeros_lru_cbuf, zeros_lru_h, *lru_args)
            rec_s, hl_s = lru_mixer(proj, mp, bs, ts, ts, _conv_tail(state_lru_conv[i]),
                                    state_lru_h[i].reshape(bs, 1, LRU_WIDTH), *lru_args)
            att = jnp.concatenate([att_p, att_s], axis=0)
            rec = jnp.concatenate([rec_p, rec_s], axis=0)
            w_out = w_out_even[i].astype(BF16)
            x = matmul_residual_ln([att, rec], [w_out[:ATT_WIDTH], w_out[ATT_WIDTH:]], x,
                                   ln_mix_g[layer], ln_mix_b[layer], tm)
            x = swiglu_residual_ln(x, jnp.zeros((m, LANES), F32), ffn_w1[i].astype(BF16), ffn_w3[i].astype(BF16),
                                   ffn_w2[i].astype(BF16), ln_ffn_g[layer], ln_ffn_b[layer], tm, 2, False)

            ks_p.append(kvt[:, :ATT_WIDTH].reshape(bp, ATT_HEADS, HEAD_DIM, sp).transpose(0, 3, 1, 2))
            vs_p.append(kvt[:, ATT_WIDTH:].reshape(bp, ATT_HEADS, HEAD_DIM, sp).transpose(0, 3, 1, 2))
            ks_s.append(k_s.reshape(bs, ts, ATT_HEADS, HEAD_DIM))
            vs_s.append(v_s.reshape(bs, ts, ATT_HEADS, HEAD_DIM))
            lruh_p.append(hl_p.reshape(bp, LRU_WIDTH))
            lruh_s.append(hl_s.reshape(bs, LRU_WIDTH))
            u_p = proj[:mp, 3 * ATT_WIDTH:3 * ATT_WIDTH + LRU_WIDTH].reshape(bp, sp, LRU_WIDTH)
            u_s = proj_s[:, 3 * ATT_WIDTH:3 * ATT_WIDTH + LRU_WIDTH].reshape(bs, ts, LRU_WIDTH)
            lrucv_p.append(u_p[:, sp - (CONV_W - 1):])
            lrucv_s.append(u_s[:, ts - (CONV_W - 1):])
        else:
            w_in = ssd_w_in[i]
            n_zx = SSD_INNER + SSD_CONV_DIM
            zx = matmul(x, w_in[:, :n_zx].astype(BF16), n_zx, tm, 2048)
            wdt = jnp.pad(w_in[:, n_zx:], ((0, 0), (0, LANES - SSD_HEADS))).astype(BF16)
            pad_h = lambda v: jnp.pad(v, (0, LANES - SSD_HEADS))
            dskip = jnp.repeat(ssd_d[i], SSD_HEAD_DIM).reshape(1, SSD_INNER)
            ssd_args = (wdt, wdt.T, pad_h(ssd_dt_bias[i]), pad_h(ssd_a_log[i]), ssd_conv_w[i],
                        ssd_conv_b[i].reshape(1, -1), dskip, ssd_norm_g[i].reshape(1, -1))
            tc_p = math.gcd(sp, SSD_CHUNK)
            y_p, st_p = ssd_mixer(x, zx, 0, bp, sp, tc_p, zeros_ssd_cbuf, zeros_ssm, *ssd_args)
            y_s, st_s = ssd_mixer(x, zx, mp, bs, ts, ts, _conv_tail(state_ssd_conv[i]),
                                  state_ssm[i].reshape(bs, SSD_HEADS * SSD_HEAD_DIM, SSD_STATE), *ssd_args)
            y = jnp.concatenate([y_p, y_s], axis=0)
            x = matmul_residual_ln([y], [ssd_w_out[i].astype(BF16)], x, ln_mix_g[layer], ln_mix_b[layer], tm)
            router_t = jnp.pad(moe_router[i].T, ((0, LANES - N_EXPERTS), (0, 0)))
            gates = router_gates(x, router_t, tm)
            x = swiglu_residual_ln(x, gates, moe_w1[i].astype(BF16), moe_w3[i].astype(BF16), moe_w2[i].astype(BF16),
                                   ln_ffn_g[layer], ln_ffn_b[layer], tm, N_EXPERTS, True)

            ssm_p.append(st_p.reshape(bp, SSD_HEADS, SSD_HEAD_DIM, SSD_STATE))
            ssm_s.append(st_s.reshape(bs, SSD_HEADS, SSD_HEAD_DIM, SSD_STATE))
            xbc_p = zx[:mp, SSD_INNER:].reshape(bp, sp, SSD_CONV_DIM)
            xbc_s = zx[mp:, SSD_INNER:].reshape(bs, ts, SSD_CONV_DIM)
            ssdcv_p.append(xbc_p[:, sp - (CONV_W - 1):])
            ssdcv_s.append(xbc_s[:, ts - (CONV_W - 1):])

    return (x[:mp].reshape(bp, sp, d), x[mp:].reshape(bs, ts, d),
            jnp.stack(ks_p), jnp.stack(vs_p), jnp.stack(ks_s), jnp.stack(vs_s),
            jnp.stack(lruh_p), jnp.stack(lruh_s), jnp.stack(lrucv_p), jnp.stack(lrucv_s),
            jnp.stack(ssm_p), jnp.stack(ssm_s), jnp.stack(ssdcv_p), jnp.stack(ssdcv_s))
```

```python
import functools
import math

import jax
import jax.numpy as jnp
from jax import lax
from jax.experimental import pallas as pl
from jax.experimental.pallas import tpu as pltpu

F32 = jnp.float32
BF16 = jnp.bfloat16

D_MODEL = 1024
DEPTH = 4
ATT_HEADS = 8
HEAD_DIM = 64
ATT_WIDTH = ATT_HEADS * HEAD_DIM
MOBA_BLOCK = 256
MOBA_TOPK = 3
LRU_WIDTH = 512
LRU_BLOCKS = 8
LRU_C = 8.0
CONV_W = 4
SSD_INNER = 2048
SSD_HEAD_DIM = 64
SSD_HEADS = 32
SSD_GROUPS = 8
SSD_HPG = 4
SSD_STATE = 128
SSD_CHUNK = 128
SSD_GROUP_W = SSD_HPG * SSD_HEAD_DIM
SSD_CONV_DIM = SSD_INNER + 2 * SSD_GROUPS * SSD_STATE
N_EXPERTS = 8
DN_ALPHA = (2 * DEPTH) ** 0.25
LN_EPS = 1e-5
RMS_EPS = 1e-6

LANES = 128
SUBLANES = 8
VMEM_LIMIT = 56 * 1024 * 1024
NEG = -1e30
HIGHEST = lax.Precision.HIGHEST


def _params(*sem):
    return pltpu.CompilerParams(dimension_semantics=sem, vmem_limit_bytes=VMEM_LIMIT)


def _dot(a, b):
    return jnp.dot(a, b, preferred_element_type=F32)


def _dot_nt(a, b, precision=None):
    return lax.dot_general(a, b, (((1,), (1,)), ((), ())), preferred_element_type=F32, precision=precision)


def _dot_tn(a, b):
    return lax.dot_general(a, b, (((0,), (0,)), ((), ())), preferred_element_type=F32)


def _sigmoid(x):
    return 1.0 / (1.0 + jnp.exp(-x))


def _silu(x):
    return x * _sigmoid(x)


def _softplus(x):
    return jnp.maximum(x, 0.0) + jnp.log1p(jnp.exp(-jnp.abs(x)))


def _gelu_tanh(x):
    return 0.5 * x * (1.0 + jnp.tanh(math.sqrt(2.0 / math.pi) * (x + 0.044715 * (x * x * x))))


def _layer_norm(y, g, b):
    mu = jnp.mean(y, axis=-1, keepdims=True)
    d = y - mu
    var = jnp.mean(d * d, axis=-1, keepdims=True)
    return d * lax.rsqrt(var + LN_EPS) * g + b


def _mm_kernel(x_ref, w_ref, o_ref):
    o_ref[...] = _dot(x_ref[...].astype(BF16), w_ref[...])


def matmul(x, w, n_cols, tm, tn):
    m, k = x.shape
    return pl.pallas_call(
        _mm_kernel,
        grid=(n_cols // tn, m // tm),
        in_specs=[pl.BlockSpec((tm, k), lambda j, i: (i, 0)),
                  pl.BlockSpec((k, tn), lambda j, i: (0, j))],
        out_specs=pl.BlockSpec((tm, tn), lambda j, i: (i, j)),
        out_shape=jax.ShapeDtypeStruct((m, n_cols), F32),
        compiler_params=_params("arbitrary", "arbitrary"),
        name="matmul",
    )(x, w)


def _mm_t_kernel(w_ref, x_ref, o_ref):
    o_ref[...] = _dot_nt(w_ref[...], x_ref[...].astype(BF16))


def matmul_t(w_t, x, n_batch, seq, ts):
    n, k = w_t.shape
    nblk = seq // ts
    return pl.pallas_call(
        _mm_t_kernel,
        grid=(n_batch, nblk),
        in_specs=[pl.BlockSpec((n, k), lambda b, s: (0, 0)),
                  pl.BlockSpec((ts, k), lambda b, s: (b * nblk + s, 0))],
        out_specs=pl.BlockSpec((None, n, ts), lambda b, s: (b, 0, s)),
        out_shape=jax.ShapeDtypeStruct((n_batch, n, seq), F32),
        compiler_params=_params("arbitrary", "arbitrary"),
        name="matmul_t",
    )(w_t, x)


def _mm_ln_kernel(*refs, n_pairs):
    a_refs = refs[:n_pairs]
    w_refs = refs[n_pairs:2 * n_pairs]
    r_ref, g_ref, b_ref, o_ref = refs[2 * n_pairs:]
    mix = _dot(a_refs[0][...].astype(BF16), w_refs[0][...])
    for a_ref, w_ref in zip(a_refs[1:], w_refs[1:]):
        mix = mix + _dot(a_ref[...].astype(BF16), w_ref[...])
    o_ref[...] = _layer_norm(DN_ALPHA * r_ref[...] + mix, g_ref[...], b_ref[...])


def matmul_residual_ln(acts, weights, resid, g, b, tm):
    m, d = resid.shape
    n_pairs = len(acts)
    in_specs = [pl.BlockSpec((tm, a.shape[1]), lambda i: (i, 0)) for a in acts]
    in_specs += [pl.BlockSpec(w.shape, lambda i: (0, 0)) for w in weights]
    in_specs += [pl.BlockSpec((tm, d), lambda i: (i, 0)),
                 pl.BlockSpec((1, d), lambda i: (0, 0)),
                 pl.BlockSpec((1, d), lambda i: (0, 0))]
    return pl.pallas_call(
        functools.partial(_mm_ln_kernel, n_pairs=n_pairs),
        grid=(m // tm,),
        in_specs=in_specs,
        out_specs=pl.BlockSpec((tm, d), lambda i: (i, 0)),
        out_shape=jax.ShapeDtypeStruct((m, d), F32),
        compiler_params=_params("arbitrary"),
        name="matmul_residual_ln",
    )(*acts, *weights, resid, g.reshape(1, d), b.reshape(1, d))


def _ffn_kernel(x_ref, gate_ref, w1_ref, w3_ref, w2_ref, g_ref, b_ref, o_ref, xb_ref, acc_ref, *, n_e, gated):
    e = pl.program_id(1)

    @pl.when(e == 0)
    def _():
        xb_ref[...] = x_ref[...].astype(BF16)
        acc_ref[...] = jnp.zeros_like(acc_ref)

    xb = xb_ref[...]
    h = _silu(_dot(xb, w1_ref[...])) * _dot(xb, w3_ref[...])
    if gated:
        lane = lax.broadcasted_iota(jnp.int32, (1, LANES), 1)
        ge = jnp.sum(jnp.where(lane == e, gate_ref[...], 0.0), axis=1, keepdims=True)
        h = h * ge
    acc_ref[...] += _dot(h.astype(BF16), w2_ref[...])

    @pl.when(e == n_e - 1)
    def _():
        o_ref[...] = _layer_norm(DN_ALPHA * x_ref[...] + acc_ref[...], g_ref[...], b_ref[...])


def swiglu_residual_ln(x, gates, w1, w3, w2, g, b, tm, n_e, gated):
    m, d = x.shape
    if gated:
        f = w1.shape[2]
        w13_spec = pl.BlockSpec((None, d, f), lambda i, e: (e, 0, 0))
        w2_spec = pl.BlockSpec((None, f, d), lambda i, e: (e, 0, 0))
    else:
        f = w1.shape[1] // n_e
        w13_spec = pl.BlockSpec((d, f), lambda i, e: (0, e))
        w2_spec = pl.BlockSpec((f, d), lambda i, e: (e, 0))
    return pl.pallas_call(
        functools.partial(_ffn_kernel, n_e=n_e, gated=gated),
        grid=(m // tm, n_e),
        in_specs=[pl.BlockSpec((tm, d), lambda i, e: (i, 0)),
                  pl.BlockSpec((tm, LANES), lambda i, e: (i, 0)),
                  w13_spec, w13_spec, w2_spec,
                  pl.BlockSpec((1, d), lambda i, e: (0, 0)),
                  pl.BlockSpec((1, d), lambda i, e: (0, 0))],
        out_specs=pl.BlockSpec((tm, d), lambda i, e: (i, 0)),
        out_shape=jax.ShapeDtypeStruct((m, d), F32),
        scratch_shapes=[pltpu.VMEM((tm, d), BF16), pltpu.VMEM((tm, d), F32)],
        compiler_params=_params("arbitrary", "arbitrary"),
        name="swiglu_residual_ln",
    )(x, gates, w1, w3, w2, g.reshape(1, d), b.reshape(1, d))


def _router_kernel(x_ref, rt_ref, o_ref):
    logits = _dot_nt(x_ref[...], rt_ref[...], precision=HIGHEST)
    lane = lax.broadcasted_iota(jnp.int32, logits.shape, 1)
    logits = jnp.where(lane < N_EXPERTS, logits, -jnp.inf)
    m1 = jnp.max(logits, axis=1, keepdims=True)
    i1 = jnp.min(jnp.where(logits == m1, lane, LANES), axis=1, keepdims=True)
    rest = jnp.where(lane == i1, -jnp.inf, logits)
    m2 = jnp.max(rest, axis=1, keepdims=True)
    i2 = jnp.min(jnp.where(rest == m2, lane, LANES), axis=1, keepdims=True)
    e2 = jnp.exp(m2 - m1)
    g1 = 1.0 / (1.0 + e2)
    g2 = e2 * g1
    o_ref[...] = jnp.where(lane == i1, g1, 0.0) + jnp.where(lane == i2, g2, 0.0)


def router_gates(x, router_t, tm):
    m, d = x.shape
    return pl.pallas_call(
        _router_kernel,
        grid=(m // tm,),
        in_specs=[pl.BlockSpec((tm, d), lambda i: (i, 0)),
                  pl.BlockSpec((LANES, d), lambda i: (0, 0))],
        out_specs=pl.BlockSpec((tm, LANES), lambda i: (i, 0)),
        out_shape=jax.ShapeDtypeStruct((m, LANES), F32),
        compiler_params=_params("arbitrary"),
        name="router_gates",
    )(x, router_t)


def _top_blocks(scores, n_valid, axis):
    blk = lax.broadcasted_iota(jnp.int32, scores.shape, axis)
    s = jnp.where(blk < n_valid, scores, -jnp.inf)
    bias = jnp.full(scores.shape, NEG, F32)
    for _ in range(MOBA_TOPK):
        mx = jnp.max(s, axis=axis, keepdims=True)
        hit = jnp.logical_and(s == mx, mx > -jnp.inf)
        idx = jnp.min(jnp.where(hit, blk, scores.shape[axis]), axis=axis, keepdims=True)
        pick = blk == idx
        bias = jnp.where(pick, 0.0, bias)
        s = jnp.where(pick, -jnp.inf, s)
    return bias


def _moba_prompt_kernel(qt_ref, kt_ref, vt_ref, o_ref, k_ref, vtb_ref, means_ref, bias_ref, s_ref, p_ref,
                        *, nb, group):
    qi = pl.program_id(2)
    blk = MOBA_BLOCK
    tq = qt_ref.shape[1]
    extents = list(range(group, nb, group)) + [nb]

    def for_extent(fn):
        lo = 0
        for nk in extents:
            pl.when(jnp.logical_and(qi >= lo, qi < nk))(functools.partial(fn, nk))
            lo = nk

    @pl.when(qi == 0)
    def _():
        for n in range(nb):
            kn = kt_ref[:, n * blk:(n + 1) * blk].T
            k_ref[n * blk:(n + 1) * blk, :] = kn.astype(BF16)
            means_ref[n:n + 1, :] = jnp.mean(kn, axis=0, keepdims=True)
        vtb_ref[...] = vt_ref[...].astype(BF16)

    qt = qt_ref[...]
    sub = lax.broadcasted_iota(jnp.int32, (LANES, 1), 0)
    blk_row = lax.broadcasted_iota(jnp.int32, (nb, tq), 0)
    ws, biases = [], []
    for h in range(2):
        head = jnp.logical_and(sub >= h * HEAD_DIM, sub < (h + 1) * HEAD_DIM)
        qth = jnp.where(head, qt, 0.0)
        scores = jnp.dot(means_ref[...], qth, preferred_element_type=F32, precision=HIGHEST)
        biases.append(jnp.where(blk_row == qi, 0.0, _top_blocks(scores, qi, 0)))
        ws.append((qth * (HEAD_DIM ** -0.5)).astype(BF16))
    w_all = jnp.concatenate(ws, axis=1)
    bias_ref[...] = jnp.concatenate(biases, axis=1)

    def logits(nk):
        s_ref[0:nk * blk, :] = _dot(k_ref[0:nk * blk, :], w_all)

    for_extent(logits)
    n_used = extents[-1]
    for nk in reversed(extents[:-1]):
        n_used = jnp.where(qi < nk, nk, n_used)

    own = pl.ds(pl.multiple_of(qi * blk, blk), blk)
    key_i = lax.broadcasted_iota(jnp.int32, (blk, 2 * tq), 0)
    q_i = lax.broadcasted_iota(jnp.int32, (blk, 2 * tq), 1) % tq
    s_ref[own, :] = jnp.where(key_i <= q_i, s_ref[own, :], NEG)

    def block_rows(jb):
        return pl.ds(pl.multiple_of(jb * blk, blk), blk)

    def pass_max(jb, m):
        return jnp.maximum(m, jnp.max(s_ref[block_rows(jb), :], axis=0, keepdims=True) + bias_ref[pl.ds(jb, 1), :])

    m = lax.fori_loop(0, n_used, pass_max, jnp.full((1, 2 * tq), NEG, F32))

    def pass_exp(jb, l):
        p = jnp.exp(s_ref[block_rows(jb), :] + (bias_ref[pl.ds(jb, 1), :] - m))
        p_ref[block_rows(jb), :] = p.astype(BF16)
        return l + jnp.sum(p, axis=0, keepdims=True)

    l = lax.fori_loop(0, n_used, pass_exp, jnp.zeros((1, 2 * tq), F32))

    def weighted_values(nk):
        acc = [_dot(vtb_ref[h * HEAD_DIM:(h + 1) * HEAD_DIM, 0:nk * blk], p_ref[0:nk * blk, h * tq:(h + 1) * tq])
               for h in range(2)]
        out_t = jnp.concatenate([acc[0] / l[:, :tq], acc[1] / l[:, tq:]], axis=0)
        o_ref[...] = out_t.T

    for_extent(weighted_values)


def moba_prompt(qkvt, n_batch, seq):
    nb = seq // MOBA_BLOCK
    tq = MOBA_BLOCK
    n_pairs = ATT_WIDTH // LANES
    return pl.pallas_call(
        functools.partial(_moba_prompt_kernel, nb=nb, group=4),
        grid=(n_batch, n_pairs, nb),
        in_specs=[pl.BlockSpec((None, LANES, tq), lambda b, p, i: (b, p, i)),
                  pl.BlockSpec((None, LANES, seq), lambda b, p, i: (b, n_pairs + p, 0)),
                  pl.BlockSpec((None, LANES, seq), lambda b, p, i: (b, 2 * n_pairs + p, 0))],
        out_specs=pl.BlockSpec((tq, LANES), lambda b, p, i: (b * nb + i, p)),
        out_shape=jax.ShapeDtypeStruct((n_batch * seq, ATT_WIDTH), F32),
        scratch_shapes=[pltpu.VMEM((seq, LANES), BF16),
                        pltpu.VMEM((LANES, seq), BF16),
                        pltpu.VMEM((nb, LANES), F32),
                        pltpu.VMEM((nb, 2 * tq), F32),
                        pltpu.VMEM((seq, 2 * tq), F32),
                        pltpu.VMEM((seq, 2 * tq), BF16)],
        compiler_params=_params("arbitrary", "arbitrary", "arbitrary"),
        name="moba_prompt",
    )(qkvt, qkvt, qkvt)


def _moba_sample_kernel(pt_ref, q_ref, ktn_ref, vn_ref, *rest, n_blocks, nbs):
    del pt_ref
    n_pages = 2 * nbs
    k_refs, v_refs = rest[:n_pages], rest[n_pages:2 * n_pages]
    o_ref, qbd_ref, ksum_ref, m_ref, l_ref, oblk_ref = rest[2 * n_pages:]
    step = pl.program_id(1)
    t_new = q_ref.shape[0]
    rows = ATT_HEADS * t_new
    lane = lax.broadcasted_iota(jnp.int32, (1, LANES), 1)

    @pl.when(step == 0)
    def _():
        q = q_ref[...]
        col_head = lax.broadcasted_iota(jnp.int32, (t_new, ATT_WIDTH), 1) // HEAD_DIM
        for h in range(ATT_HEADS):
            qbd_ref[h * t_new:(h + 1) * t_new, :] = jnp.where(col_head == h, q, 0.0)
        ksum_ref[...] = jnp.zeros_like(ksum_ref)
        m_ref[...] = jnp.zeros_like(m_ref)
        l_ref[...] = jnp.zeros_like(l_ref)

    qb = (qbd_ref[...] * (HEAD_DIM ** -0.5)).astype(BF16)
    m_all, l_all, ksum_all = m_ref[...], l_ref[...], ksum_ref[...]
    for j in range(nbs):
        n = step * nbs + j
        kt = jnp.concatenate([k_refs[2 * j][...].reshape(ATT_WIDTH, LANES),
                              k_refs[2 * j + 1][...].reshape(ATT_WIDTH, LANES)], axis=1)
        vt = jnp.concatenate([v_refs[2 * j][...].reshape(ATT_WIDTH, LANES),
                              v_refs[2 * j + 1][...].reshape(ATT_WIDTH, LANES)], axis=1)
        logits = _dot(qb, kt.astype(BF16))
        mx = jnp.max(logits, axis=1, keepdims=True)
        p = jnp.exp(logits - mx)
        oblk_ref[n] = _dot_nt(p.astype(BF16), vt.astype(BF16))
        m_all = jnp.where(lane == n, mx, m_all)
        l_all = jnp.where(lane == n, jnp.sum(p, axis=1, keepdims=True), l_all)
        ksum_all = jnp.where(lane == n, jnp.sum(kt, axis=1, keepdims=True), ksum_all)
    m_ref[...] = m_all
    l_ref[...] = l_all
    ksum_ref[...] = ksum_all

    @pl.when(step == n_blocks // nbs - 1)
    def _():
        qbd = qbd_ref[...]
        scores = jnp.dot(qbd, ksum_all * (1.0 / MOBA_BLOCK), preferred_element_type=F32,
                         precision=HIGHEST)
        bias = _top_blocks(scores, n_blocks, 1)
        lo = _dot(qb, ktn_ref[...].astype(BF16))
        r_q = lax.broadcasted_iota(jnp.int32, (rows, t_new), 0) % t_new
        c_t = lax.broadcasted_iota(jnp.int32, (rows, t_new), 1)
        lo = jnp.where(c_t <= r_q, lo, NEG)
        m_own = jnp.max(lo, axis=1, keepdims=True)
        p_own = jnp.exp(lo - m_own)
        l_own = jnp.sum(p_own, axis=1, keepdims=True)
        o_own = _dot(p_own.astype(BF16), vn_ref[...].astype(BF16))
        m_sel = m_all + bias
        m_tot = jnp.maximum(jnp.max(m_sel, axis=1, keepdims=True), m_own)
        w = jnp.exp(m_sel - m_tot)
        w_own = jnp.exp(m_own - m_tot)
        l_tot = jnp.sum(w * l_all, axis=1, keepdims=True) + w_own * l_own
        o_tot = w_own * o_own
        for j in range(n_blocks):
            o_tot = o_tot + w[:, j:j + 1] * oblk_ref[j]
        o_tot = o_tot / l_tot
        col_head = lax.broadcasted_iota(jnp.int32, (t_new, ATT_WIDTH), 1) // HEAD_DIM
        out = jnp.zeros((t_new, ATT_WIDTH), F32)
        for h in range(ATT_HEADS):
            out = jnp.where(col_head == h, o_tot[h * t_new:(h + 1) * t_new, :], out)
        o_ref[...] = out


def moba_sample(qkv_s, ktn, cache_kt, cache_vt, page_table, layer, t_new):
    n_seq, n_pages = page_table.shape
    assert MOBA_BLOCK == 2 * cache_kt.shape[-1]
    n_blocks = n_pages // 2
    nbs = 4 if n_blocks % 4 == 0 else 1
    rows = ATT_HEADS * t_new
    page_shape = (None, None) + cache_kt.shape[2:]

    def page_spec(j):
        return pl.BlockSpec(page_shape, lambda b, s, pt: (layer, pt[b, 2 * nbs * s + j], 0, 0, 0))

    pages = [page_spec(j) for j in range(2 * nbs)]
    grid_spec = pltpu.PrefetchScalarGridSpec(
        num_scalar_prefetch=1,
        grid=(n_seq, n_blocks // nbs),
        in_specs=[pl.BlockSpec((t_new, ATT_WIDTH), lambda b, s, pt: (b, 0)),
                  pl.BlockSpec((None, ATT_WIDTH, t_new), lambda b, s, pt: (b, 0, 0)),
                  pl.BlockSpec((t_new, ATT_WIDTH), lambda b, s, pt: (b, 2))] + pages + pages,
        out_specs=pl.BlockSpec((t_new, ATT_WIDTH), lambda b, s, pt: (b, 0)),
        scratch_shapes=[pltpu.VMEM((rows, ATT_WIDTH), F32),
                        pltpu.VMEM((ATT_WIDTH, LANES), F32),
                        pltpu.VMEM((rows, LANES), F32), pltpu.VMEM((rows, LANES), F32),
                        pltpu.VMEM((n_blocks, rows, ATT_WIDTH), F32)],
    )
    return pl.pallas_call(
        functools.partial(_moba_sample_kernel, n_blocks=n_blocks, nbs=nbs),
        grid_spec=grid_spec,
        out_shape=jax.ShapeDtypeStruct((n_seq * t_new, ATT_WIDTH), F32),
        compiler_params=_params("arbitrary", "arbitrary"),
        name="moba_sample",
    )(page_table, qkv_s, ktn, qkv_s, *([cache_kt] * (2 * nbs)), *([cache_vt] * (2 * nbs)))


def _shift_rows(x, d, fill):
    row = lax.broadcasted_iota(jnp.int32, x.shape, 0)
    return jnp.where(row >= d, pltpu.roll(x, d, 0), fill)


def _lru_kernel(u_ref, gate_ref, cbuf_ref, h0_ref, cw_ref, cb_ref, wa_ref, ba_ref, wx_ref, bx_ref, lam_ref,
                rec_ref, hl_ref, xbuf, hc, *, tc, n_chunks):
    c = pl.program_id(1)

    @pl.when(c == 0)
    def _():
        xbuf[0:SUBLANES, :] = cbuf_ref[...]
        hc[...] = h0_ref[...]

    xbuf[SUBLANES:SUBLANES + tc, :] = u_ref[...]
    w = cw_ref[...]
    uc = cb_ref[...]
    for t in range(CONV_W):
        off = SUBLANES - (CONV_W - 1) + t
        uc = uc + xbuf[off:off + tc, :] * w[t:t + 1, :]
    xbuf[0:SUBLANES, :] = xbuf[tc:tc + SUBLANES, :]

    ub = uc.astype(BF16)
    gate_r = _sigmoid(_dot(ub, wa_ref[...]) + ba_ref[...])
    gate_i = _sigmoid(_dot(ub, wx_ref[...]) + bx_ref[...])
    log_a = -LRU_C * gate_r * _softplus(-lam_ref[...])
    a = jnp.exp(log_a)
    bterm = jnp.sqrt(-jnp.tanh(log_a) * (a * a + 1.0)) * (gate_i * uc)
    row = lax.broadcasted_iota(jnp.int32, a.shape, 0)
    bterm = bterm + jnp.where(row == 0, a * hc[...], 0.0)
    d = 1
    while d < tc:
        b_sh = _shift_rows(bterm, d, 0.0)
        a_sh = _shift_rows(a, d, 1.0)
        bterm = a * b_sh + bterm
        a = a * a_sh
        d *= 2
    h = bterm
    hc[...] = h[tc - 1:tc, :]
    rec_ref[...] = h * _gelu_tanh(gate_ref[...])

    @pl.when(c == n_chunks - 1)
    def _():
        hl_ref[...] = h[tc - 1:tc, :]


def lru_mixer(proj, row0, n_batch, seq, tc, cbuf, h0, cw, cb, wa, ba, wx, bx, lam):
    n_chunks = seq // tc
    blk0 = row0 // tc
    vec = lambda: pl.BlockSpec((1, LRU_WIDTH), lambda b, c: (0, 0))
    return pl.pallas_call(
        functools.partial(_lru_kernel, tc=tc, n_chunks=n_chunks),
        grid=(n_batch, n_chunks),
        in_specs=[pl.BlockSpec((tc, LRU_WIDTH), lambda b, c: (blk0 + b * n_chunks + c, 0)),
                  pl.BlockSpec((tc, LRU_WIDTH), lambda b, c: (blk0 + b * n_chunks + c, 1)),
                  pl.BlockSpec((None, SUBLANES, LRU_WIDTH), lambda b, c: (b, 0, 0)),
                  pl.BlockSpec((None, 1, LRU_WIDTH), lambda b, c: (b, 0, 0)),
                  pl.BlockSpec((CONV_W, LRU_WIDTH), lambda b, c: (0, 0)),
                  vec(),
                  pl.BlockSpec((LRU_WIDTH, LRU_WIDTH), lambda b, c: (0, 0)),
                  vec(),
                  pl.BlockSpec((LRU_WIDTH, LRU_WIDTH), lambda b, c: (0, 0)),
                  vec(), vec()],
        out_specs=[pl.BlockSpec((tc, LRU_WIDTH), lambda b, c: (b * n_chunks + c, 0)),
                   pl.BlockSpec((None, 1, LRU_WIDTH), lambda b, c: (b, 0, 0))],
        out_shape=[jax.ShapeDtypeStruct((n_batch * seq, LRU_WIDTH), F32),
                   jax.ShapeDtypeStruct((n_batch, 1, LRU_WIDTH), F32)],
        scratch_shapes=[pltpu.VMEM((SUBLANES + tc, LRU_WIDTH), F32), pltpu.VMEM((1, LRU_WIDTH), F32)],
        compiler_params=_params("arbitrary", "arbitrary"),
        name="lru_mixer",
    )(proj, proj, cbuf, h0, cw, cb, wa, ba, wx, bx, lam)


def _expand_heads(v, g):
    lane = lax.broadcasted_iota(jnp.int32, (1, SSD_GROUP_W), 1)
    out = v[:, SSD_HPG * g + SSD_HPG - 1:SSD_HPG * g + SSD_HPG]
    for k in range(SSD_HPG - 2, -1, -1):
        out = jnp.where(lane < (k + 1) * SSD_HEAD_DIM, v[:, SSD_HPG * g + k:SSD_HPG * g + k + 1], out)
    return out


def _ssd_kernel(x_ref, zx_ref, cbuf_ref, h0_ref, wdt_ref, wdtt_ref, dtb_ref, dtbt_ref, alog_ref, alogt_ref,
                cw_ref, cb_ref, dskip_ref, ng_ref, y_ref, hl_ref, xbuf, st_ref, *, tc, n_chunks):
    c = pl.program_id(1)

    @pl.when(c == 0)
    def _():
        xbuf[0:SUBLANES, :] = cbuf_ref[...]
        for g in range(SSD_GROUPS):
            st_ref[g] = h0_ref[g * SSD_GROUP_W:(g + 1) * SSD_GROUP_W, :].T

    xbuf[SUBLANES:SUBLANES + tc, :] = zx_ref[:, SSD_INNER:]
    cw = cw_ref[...]
    cb = cb_ref[...]

    def conv_silu(lo, width):
        acc = cb[:, lo:lo + width]
        for t in range(CONV_W):
            off = SUBLANES - (CONV_W - 1) + t
            acc = acc + xbuf[off:off + tc, lo:lo + width] * cw[t:t + 1, lo:lo + width]
        return _silu(acc)

    xb = x_ref[...].astype(BF16)
    dt = _softplus(_dot(xb, wdt_ref[...]) + dtb_ref[...])
    dtt = _softplus(_dot_nt(wdtt_ref[...], xb) + dtbt_ref[...])
    a_neg = -jnp.exp(alog_ref[...])
    a_negt = -jnp.exp(alogt_ref[...])
    r = lax.broadcasted_iota(jnp.int32, (tc, tc), 0)
    s = lax.broadcasted_iota(jnp.int32, (tc, tc), 1)
    causal = s <= r
    tri = causal.astype(F32)
    cs = jnp.dot(tri, dt * a_neg, preferred_element_type=F32, precision=HIGHEST)
    cst = _dot_nt(dtt * a_negt, tri, precision=HIGHEST)
    cs_end = cs[tc - 1:tc, :]
    w_end = jnp.exp(cs_end - cs) * dt
    ecs = jnp.exp(cs)
    chunk_decay = jnp.exp(cs_end)
    lane = lax.broadcasted_iota(jnp.int32, (1, SSD_GROUP_W), 1)

    for g in range(SSD_GROUPS):
        xg = conv_silu(g * SSD_GROUP_W, SSD_GROUP_W)
        bg = conv_silu(SSD_INNER + g * SSD_STATE, SSD_STATE).astype(BF16)
        cg = conv_silu(SSD_INNER + SSD_GROUPS * SSD_STATE + g * SSD_STATE, SSD_STATE).astype(BF16)
        cbm = _dot_nt(cg, bg)
        y = jnp.zeros((tc, SSD_GROUP_W), F32)
        for k in range(SSD_HPG):
            e = SSD_HPG * g + k
            dec = jnp.exp(jnp.where(causal, cs[:, e:e + 1] - cst[e:e + 1, :], -jnp.inf))
            m = (cbm * dec * dtt[e:e + 1, :]).astype(BF16)
            head = jnp.logical_and(lane >= k * SSD_HEAD_DIM, lane < (k + 1) * SSD_HEAD_DIM)
            y = y + _dot(m, jnp.where(head, xg, 0.0).astype(BF16))
        st = st_ref[g]
        y = y + _dot(cg, st.astype(BF16)) * _expand_heads(ecs, g)
        xw = (xg * _expand_heads(w_end, g)).astype(BF16)
        st_ref[g] = st * _expand_heads(chunk_decay, g) + _dot_tn(bg, xw)
        y = y + dskip_ref[:, g * SSD_GROUP_W:(g + 1) * SSD_GROUP_W] * xg
        y = y * _silu(zx_ref[:, g * SSD_GROUP_W:(g + 1) * SSD_GROUP_W])
        y = y * lax.rsqrt(jnp.mean(y * y, axis=-1, keepdims=True) + RMS_EPS)
        y_ref[:, g * SSD_GROUP_W:(g + 1) * SSD_GROUP_W] = y * ng_ref[:, g * SSD_GROUP_W:(g + 1) * SSD_GROUP_W]

    xbuf[0:SUBLANES, :] = xbuf[tc:tc + SUBLANES, :]

    @pl.when(c == n_chunks - 1)
    def _():
        for g in range(SSD_GROUPS):
            hl_ref[g * SSD_GROUP_W:(g + 1) * SSD_GROUP_W, :] = st_ref[g].T


def ssd_mixer(x, zx, row0, n_batch, seq, tc, cbuf, h0, wdt, wdtt, dtb, alog, cw, cb, dskip, ng):
    n_chunks = seq // tc
    blk0 = row0 // tc
    m_state = SSD_HEADS * SSD_HEAD_DIM
    const2 = lambda shape: pl.BlockSpec(shape, lambda b, c: (0, 0))
    return pl.pallas_call(
        functools.partial(_ssd_kernel, tc=tc, n_chunks=n_chunks),
        grid=(n_batch, n_chunks),
        in_specs=[pl.BlockSpec((tc, D_MODEL), lambda b, c: (blk0 + b * n_chunks + c, 0)),
                  pl.BlockSpec((tc, SSD_INNER + SSD_CONV_DIM), lambda b, c: (blk0 + b * n_chunks + c, 0)),
                  pl.BlockSpec((None, SUBLANES, SSD_CONV_DIM), lambda b, c: (b, 0, 0)),
                  pl.BlockSpec((None, m_state, SSD_STATE), lambda b, c: (b, 0, 0)),
                  const2((D_MODEL, LANES)), const2((LANES, D_MODEL)),
                  const2((1, LANES)), const2((LANES, 1)), const2((1, LANES)), const2((LANES, 1)),
                  const2((CONV_W, SSD_CONV_DIM)), const2((1, SSD_CONV_DIM)),
                  const2((1, SSD_INNER)), const2((1, SSD_INNER))],
        out_specs=[pl.BlockSpec((tc, SSD_INNER), lambda b, c: (b * n_chunks + c, 0)),
                   pl.BlockSpec((None, m_state, SSD_STATE), lambda b, c: (b, 0, 0))],
        out_shape=[jax.ShapeDtypeStruct((n_batch * seq, SSD_INNER), F32),
                   jax.ShapeDtypeStruct((n_batch, m_state, SSD_STATE), F32)],
        scratch_shapes=[pltpu.VMEM((SUBLANES + tc, SSD_CONV_DIM), F32),
                        pltpu.VMEM((SSD_GROUPS, SSD_STATE, SSD_GROUP_W), F32)],
        compiler_params=_params("arbitrary", "arbitrary"),
        name="ssd_mixer",
    )(x, zx, cbuf, h0, wdt, wdtt, dtb.reshape(1, LANES), dtb.reshape(LANES, 1),
      alog.reshape(1, LANES), alog.reshape(LANES, 1), cw, cb, dskip, ng)


def _block_diag(w):
    n, d, e = w.shape
    eye = jnp.eye(n, dtype=w.dtype)
    return (eye[:, None, :, None] * w[:, :, None, :]).reshape(n * d, n * e)


def _conv_tail(buf):
    return jnp.pad(buf, ((0, 0), (SUBLANES - (CONV_W - 1), 0), (0, 0)))


def _last_rows(a, row0, n_batch, seq, col0, width):
    rows = [lax.slice(a, (row0 + b * seq + seq - (CONV_W - 1), col0), (row0 + (b + 1) * seq, col0 + width))
            for b in range(n_batch)]
    return jnp.stack(rows)


def kernel(x_prompt, x_sample, cache_k, cache_v, page_table, state_lru_h, state_lru_conv, state_ssm, state_ssd_conv, w_in_even, lru_conv_w, lru_conv_b, lru_w_a, lru_b_a, lru_w_x, lru_b_x, lru_lambda, w_out_even, ssd_w_in, ssd_conv_w, ssd_conv_b, ssd_dt_bias, ssd_a_log, ssd_d, ssd_norm_g, ssd_w_out, ffn_w1, ffn_w3, ffn_w2, moe_router, moe_w1, moe_w3, moe_w2, ln_mix_g, ln_mix_b, ln_ffn_g, ln_ffn_b):
    bp, sp, d = x_prompt.shape
    bs, ts, _ = x_sample.shape
    mp = bp * sp
    ms = bs * ts
    m = mp + ms
    tm = 640 if m % 640 == 0 else ts * 8
    x = jnp.concatenate([x_prompt.reshape(mp, d), x_sample.reshape(ms, d)], axis=0)

    cache_kt = cache_k.transpose(0, 1, 3, 4, 2)
    cache_vt = cache_v.transpose(0, 1, 3, 4, 2)

    ks_p, vs_p, ks_s, vs_s = [], [], [], []
    lruh_p, lruh_s, lrucv_p, lrucv_s = [], [], [], []
    ssm_p, ssm_s, ssdcv_p, ssdcv_s = [], [], [], []
    zeros_lru_cbuf = jnp.zeros((bp, SUBLANES, LRU_WIDTH), F32)
    zeros_lru_h = jnp.zeros((bp, 1, LRU_WIDTH), F32)
    zeros_ssd_cbuf = jnp.zeros((bp, SUBLANES, SSD_CONV_DIM), F32)
    zeros_ssm = jnp.zeros((bp, SSD_HEADS * SSD_HEAD_DIM, SSD_STATE), F32)

    for layer in range(DEPTH):
        i = layer // 2
        if layer % 2 == 0:
            n_qkv = 3 * ATT_WIDTH
            w_in = w_in_even[i].astype(BF16)
            w_qkv = w_in[:, :n_qkv]
            qkvt = matmul_t(w_qkv.T, x, bp, sp, 512 if sp % 512 == 0 else sp)
            proj = matmul(x, w_in[:, n_qkv:], 2 * LRU_WIDTH, tm, 2 * LRU_WIDTH)
            qkv_s = matmul(x[mp:], w_qkv, n_qkv, ms, n_qkv)
            att_p = moba_prompt(qkvt, bp, sp)
            k_s = qkv_s[:, ATT_WIDTH:2 * ATT_WIDTH]
            v_s = qkv_s[:, 2 * ATT_WIDTH:]
            ktn = k_s.reshape(bs, ts, ATT_WIDTH).transpose(0, 2, 1)
            att_s = moba_sample(qkv_s, ktn, cache_kt, cache_vt, page_table, i, ts)

            wa = _block_diag(lru_w_a[i]).astype(BF16)
            wx = _block_diag(lru_w_x[i]).astype(BF16)
            lru_args = (lru_conv_w[i], lru_conv_b[i].reshape(1, -1), wa, lru_b_a[i].reshape(1, -1),
                        wx, lru_b_x[i].reshape(1, -1), lru_lambda[i].reshape(1, -1))
            tc_p = 256 if sp % 256 == 0 else sp
            rec_p, hl_p = lru_mixer(proj, 0, bp, sp, tc_p, zeros_lru_cbuf, zeros_lru_h, *lru_args)
            rec_s, hl_s = lru_mixer(proj, mp, bs, ts, ts, _conv_tail(state_lru_conv[i]),
                                    state_lru_h[i].reshape(bs, 1, LRU_WIDTH), *lru_args)
            att = jnp.concatenate([att_p, att_s], axis=0)
            rec = jnp.concatenate([rec_p, rec_s], axis=0)
            w_out = w_out_even[i].astype(BF16)
            x = matmul_residual_ln([att, rec], [w_out[:ATT_WIDTH], w_out[ATT_WIDTH:]], x,
                                   ln_mix_g[layer], ln_mix_b[layer], tm)
            x = swiglu_residual_ln(x, jnp.zeros((m, LANES), F32), ffn_w1[i].astype(BF16), ffn_w3[i].astype(BF16),
                                   ffn_w2[i].astype(BF16), ln_ffn_g[layer], ln_ffn_b[layer], tm, 2, False)

            head_major = lambda a: a.reshape(bp, ATT_HEADS, HEAD_DIM, sp).transpose(0, 3, 1, 2)
            ks_p.append(head_major(qkvt[:, ATT_WIDTH:2 * ATT_WIDTH]))
            vs_p.append(head_major(qkvt[:, 2 * ATT_WIDTH:]))
            ks_s.append(k_s.reshape(bs, ts, ATT_HEADS, HEAD_DIM))
            vs_s.append(v_s.reshape(bs, ts, ATT_HEADS, HEAD_DIM))
            lruh_p.append(hl_p.reshape(bp, LRU_WIDTH))
            lruh_s.append(hl_s.reshape(bs, LRU_WIDTH))
            lrucv_p.append(_last_rows(proj, 0, bp, sp, 0, LRU_WIDTH))
            lrucv_s.append(_last_rows(proj, mp, bs, ts, 0, LRU_WIDTH))
        else:
            w_in = ssd_w_in[i]
            n_zx = SSD_INNER + SSD_CONV_DIM
            zx = matmul(x, w_in[:, :n_zx].astype(BF16), n_zx, tm, 2048)
            wdt = jnp.pad(w_in[:, n_zx:], ((0, 0), (0, LANES - SSD_HEADS))).astype(BF16)
            pad_h = lambda v: jnp.pad(v, (0, LANES - SSD_HEADS))
            dskip = jnp.repeat(ssd_d[i], SSD_HEAD_DIM).reshape(1, SSD_INNER)
            ssd_args = (wdt, wdt.T, pad_h(ssd_dt_bias[i]), pad_h(ssd_a_log[i]), ssd_conv_w[i],
                        ssd_conv_b[i].reshape(1, -1), dskip, ssd_norm_g[i].reshape(1, -1))
            tc_p = math.gcd(sp, SSD_CHUNK)
            y_p, st_p = ssd_mixer(x, zx, 0, bp, sp, tc_p, zeros_ssd_cbuf, zeros_ssm, *ssd_args)
            y_s, st_s = ssd_mixer(x, zx, mp, bs, ts, ts, _conv_tail(state_ssd_conv[i]),
                                  state_ssm[i].reshape(bs, SSD_HEADS * SSD_HEAD_DIM, SSD_STATE), *ssd_args)
            y = jnp.concatenate([y_p, y_s], axis=0)
            x = matmul_residual_ln([y], [ssd_w_out[i].astype(BF16)], x, ln_mix_g[layer], ln_mix_b[layer], tm)
            router_t = jnp.pad(moe_router[i].T, ((0, LANES - N_EXPERTS), (0, 0)))
            gates = router_gates(x, router_t, tm)
            x = swiglu_residual_ln(x, gates, moe_w1[i].astype(BF16), moe_w3[i].astype(BF16), moe_w2[i].astype(BF16),
                                   ln_ffn_g[layer], ln_ffn_b[layer], tm, N_EXPERTS, True)

            ssm_p.append(st_p.reshape(bp, SSD_HEADS, SSD_HEAD_DIM, SSD_STATE))
            ssm_s.append(st_s.reshape(bs, SSD_HEADS, SSD_HEAD_DIM, SSD_STATE))
            ssdcv_p.append(_last_rows(zx, 0, bp, sp, SSD_INNER, SSD_CONV_DIM))
            ssdcv_s.append(_last_rows(zx, mp, bs, ts, SSD_INNER, SSD_CONV_DIM))

    return (x[:mp].reshape(bp, sp, d), x[mp:].reshape(bs, ts, d),
            jnp.stack(ks_p), jnp.stack(vs_p), jnp.stack(ks_s), jnp.stack(vs_s),
            jnp.stack(lruh_p), jnp.stack(lruh_s), jnp.stack(lrucv_p), jnp.stack(lrucv_s),
            jnp.stack(ssm_p), jnp.stack(ssm_s), jnp.stack(ssdcv_p), jnp.stack(ssdcv_s))
```

```python
import functools
import math

import jax
import jax.numpy as jnp
from jax import lax
from jax.experimental import pallas as pl
from jax.experimental.pallas import tpu as pltpu

F32 = jnp.float32
BF16 = jnp.bfloat16

D_MODEL = 1024
DEPTH = 4
ATT_HEADS = 8
HEAD_DIM = 64
ATT_WIDTH = ATT_HEADS * HEAD_DIM
MOBA_BLOCK = 256
MOBA_TOPK = 3
LRU_WIDTH = 512
LRU_BLOCKS = 8
LRU_C = 8.0
CONV_W = 4
SSD_INNER = 2048
SSD_HEAD_DIM = 64
SSD_HEADS = 32
SSD_GROUPS = 8
SSD_HPG = 4
SSD_STATE = 128
SSD_CHUNK = 128
SSD_GROUP_W = SSD_HPG * SSD_HEAD_DIM
SSD_CONV_DIM = SSD_INNER + 2 * SSD_GROUPS * SSD_STATE
N_EXPERTS = 8
DN_ALPHA = (2 * DEPTH) ** 0.25
LN_EPS = 1e-5
RMS_EPS = 1e-6

LANES = 128
SUBLANES = 8
VMEM_LIMIT = 56 * 1024 * 1024
NEG = -1e30
HIGHEST = lax.Precision.HIGHEST


def _params(*sem):
    return pltpu.CompilerParams(dimension_semantics=sem, vmem_limit_bytes=VMEM_LIMIT)


def _dot(a, b):
    return jnp.dot(a, b, preferred_element_type=F32)


def _dot_nt(a, b, precision=None):
    return lax.dot_general(a, b, (((1,), (1,)), ((), ())), preferred_element_type=F32, precision=precision)


def _dot_tn(a, b):
    return lax.dot_general(a, b, (((0,), (0,)), ((), ())), preferred_element_type=F32)


def _sigmoid(x):
    return 1.0 / (1.0 + jnp.exp(-x))


def _silu(x):
    return x * _sigmoid(x)


def _softplus(x):
    return jnp.maximum(x, 0.0) + jnp.log1p(jnp.exp(-jnp.abs(x)))


def _gelu_tanh(x):
    return 0.5 * x * (1.0 + jnp.tanh(math.sqrt(2.0 / math.pi) * (x + 0.044715 * (x * x * x))))


def _layer_norm(y, g, b):
    mu = jnp.mean(y, axis=-1, keepdims=True)
    d = y - mu
    var = jnp.mean(d * d, axis=-1, keepdims=True)
    return d * lax.rsqrt(var + LN_EPS) * g + b


def _mm_kernel(x_ref, w_ref, o_ref):
    o_ref[...] = _dot(x_ref[...].astype(BF16), w_ref[...])


def matmul(x, w, n_cols, tm, tn):
    m, k = x.shape
    return pl.pallas_call(
        _mm_kernel,
        grid=(n_cols // tn, m // tm),
        in_specs=[pl.BlockSpec((tm, k), lambda j, i: (i, 0)),
                  pl.BlockSpec((k, tn), lambda j, i: (0, j))],
        out_specs=pl.BlockSpec((tm, tn), lambda j, i: (i, j)),
        out_shape=jax.ShapeDtypeStruct((m, n_cols), F32),
        compiler_params=_params("arbitrary", "arbitrary"),
        name="matmul",
    )(x, w)


def _qkv_t_kernel(w_ref, x_ref, *rest):
    q_ref, k_ref, v_ref = rest[-3:]
    res = _dot_nt(w_ref[...], x_ref[...].astype(BF16))
    q_ref[...] = res[:ATT_WIDTH]
    k_ref[...] = res[ATT_WIDTH:2 * ATT_WIDTH]
    v_ref[...] = res[2 * ATT_WIDTH:]


def qkv_transposed(w_t, x, n_batch, seq, ts, layer, n_layers, kv_prev):
    n, k = w_t.shape
    nblk = seq // ts
    kv_shape = jax.ShapeDtypeStruct((n_layers, n_batch, ATT_WIDTH, seq), F32)
    kv_spec = pl.BlockSpec((None, None, ATT_WIDTH, ts), lambda b, s: (layer, b, 0, s))
    carried = [] if kv_prev is None else list(kv_prev)
    return pl.pallas_call(
        _qkv_t_kernel,
        grid=(n_batch, nblk),
        in_specs=[pl.BlockSpec((n, k), lambda b, s: (0, 0)),
                  pl.BlockSpec((ts, k), lambda b, s: (b * nblk + s, 0))]
                 + [pl.BlockSpec(memory_space=pl.ANY)] * len(carried),
        out_specs=[pl.BlockSpec((None, ATT_WIDTH, ts), lambda b, s: (b, 0, s)), kv_spec, kv_spec],
        out_shape=[jax.ShapeDtypeStruct((n_batch, ATT_WIDTH, seq), F32), kv_shape, kv_shape],
        input_output_aliases={2 + j: 1 + j for j in range(len(carried))},
        compiler_params=_params("arbitrary", "arbitrary"),
        name="qkv_transposed",
    )(w_t, x, *carried)


def _top2_gates(logits):
    lane = lax.broadcasted_iota(jnp.int32, logits.shape, 1)
    logits = jnp.where(lane < N_EXPERTS, logits, -jnp.inf)
    m1 = jnp.max(logits, axis=1, keepdims=True)
    i1 = jnp.min(jnp.where(logits == m1, lane, LANES), axis=1, keepdims=True)
    rest = jnp.where(lane == i1, -jnp.inf, logits)
    m2 = jnp.max(rest, axis=1, keepdims=True)
    i2 = jnp.min(jnp.where(rest == m2, lane, LANES), axis=1, keepdims=True)
    e2 = jnp.exp(m2 - m1)
    g1 = 1.0 / (1.0 + e2)
    g2 = e2 * g1
    return jnp.where(lane == i1, g1, 0.0) + jnp.where(lane == i2, g2, 0.0)


def _mm_ln_kernel(*refs, n_pairs, routed):
    a_refs = refs[:n_pairs]
    w_refs = refs[n_pairs:2 * n_pairs]
    r_ref, g_ref, b_ref = refs[2 * n_pairs:2 * n_pairs + 3]
    mix = _dot(a_refs[0][...].astype(BF16), w_refs[0][...])
    for a_ref, w_ref in zip(a_refs[1:], w_refs[1:]):
        mix = mix + _dot(a_ref[...].astype(BF16), w_ref[...])
    y = _layer_norm(DN_ALPHA * r_ref[...] + mix, g_ref[...], b_ref[...])
    if routed:
        rt_ref, o_ref, gates_ref = refs[2 * n_pairs + 3:]
        gates_ref[...] = _top2_gates(_dot_nt(y, rt_ref[...], precision=HIGHEST))
    else:
        o_ref, = refs[2 * n_pairs + 3:]
    o_ref[...] = y


def matmul_residual_ln(acts, weights, resid, g, b, tm, router_t=None):
    m, d = resid.shape
    n_pairs = len(acts)
    routed = router_t is not None
    in_specs = [pl.BlockSpec((tm, a.shape[1]), lambda i: (i, 0)) for a in acts]
    in_specs += [pl.BlockSpec(w.shape, lambda i: (0, 0)) for w in weights]
    in_specs += [pl.BlockSpec((tm, d), lambda i: (i, 0)),
                 pl.BlockSpec((1, d), lambda i: (0, 0)),
                 pl.BlockSpec((1, d), lambda i: (0, 0))]
    out_specs = [pl.BlockSpec((tm, d), lambda i: (i, 0))]
    out_shape = [jax.ShapeDtypeStruct((m, d), F32)]
    extra = []
    if routed:
        in_specs.append(pl.BlockSpec((LANES, d), lambda i: (0, 0)))
        out_specs.append(pl.BlockSpec((tm, LANES), lambda i: (i, 0)))
        out_shape.append(jax.ShapeDtypeStruct((m, LANES), F32))
        extra = [router_t]
    out = pl.pallas_call(
        functools.partial(_mm_ln_kernel, n_pairs=n_pairs, routed=routed),
        grid=(m // tm,),
        in_specs=in_specs,
        out_specs=out_specs,
        out_shape=out_shape,
        compiler_params=_params("arbitrary"),
        name="matmul_residual_ln",
    )(*acts, *weights, resid, g.reshape(1, d), b.reshape(1, d), *extra)
    return out if routed else out[0]


def _ffn_kernel(x_ref, gate_ref, w1_ref, w3_ref, w2_ref, g_ref, b_ref, o_ref, xb_ref, acc_ref, *, n_e, gated):
    e = pl.program_id(1)

    @pl.when(e == 0)
    def _():
        xb_ref[...] = x_ref[...].astype(BF16)
        acc_ref[...] = jnp.zeros_like(acc_ref)

    xb = xb_ref[...]
    h = _silu(_dot(xb, w1_ref[...])) * _dot(xb, w3_ref[...])
    if gated:
        lane = lax.broadcasted_iota(jnp.int32, (1, LANES), 1)
        ge = jnp.sum(jnp.where(lane == e, gate_ref[...], 0.0), axis=1, keepdims=True)
        h = h * ge
    acc_ref[...] += _dot(h.astype(BF16), w2_ref[...])

    @pl.when(e == n_e - 1)
    def _():
        o_ref[...] = _layer_norm(DN_ALPHA * x_ref[...] + acc_ref[...], g_ref[...], b_ref[...])


def swiglu_residual_ln(x, gates, w1, w3, w2, layer, g, b, tm, n_e, gated):
    m, d = x.shape
    if gated:
        f = w1.shape[3]
        w13_spec = pl.BlockSpec((None, None, d, f), lambda i, e: (layer, e, 0, 0))
        w2_spec = pl.BlockSpec((None, None, f, d), lambda i, e: (layer, e, 0, 0))
    else:
        f = w1.shape[2] // n_e
        w13_spec = pl.BlockSpec((None, d, f), lambda i, e: (layer, 0, e))
        w2_spec = pl.BlockSpec((None, f, d), lambda i, e: (layer, e, 0))
    return pl.pallas_call(
        functools.partial(_ffn_kernel, n_e=n_e, gated=gated),
        grid=(m // tm, n_e),
        in_specs=[pl.BlockSpec((tm, d), lambda i, e: (i, 0)),
                  pl.BlockSpec((tm, LANES), lambda i, e: (i, 0)),
                  w13_spec, w13_spec, w2_spec,
                  pl.BlockSpec((1, d), lambda i, e: (0, 0)),
                  pl.BlockSpec((1, d), lambda i, e: (0, 0))],
        out_specs=pl.BlockSpec((tm, d), lambda i, e: (i, 0)),
        out_shape=jax.ShapeDtypeStruct((m, d), F32),
        scratch_shapes=[pltpu.VMEM((tm, d), BF16), pltpu.VMEM((tm, d), F32)],
        compiler_params=_params("arbitrary", "arbitrary"),
        name="swiglu_residual_ln",
    )(x, gates, w1, w3, w2, g.reshape(1, d), b.reshape(1, d))


def _top_blocks(scores, n_valid, axis):
    blk = lax.broadcasted_iota(jnp.int32, scores.shape, axis)
    s = jnp.where(blk < n_valid, scores, -jnp.inf)
    bias = jnp.full(scores.shape, NEG, F32)
    for _ in range(MOBA_TOPK):
        mx = jnp.max(s, axis=axis, keepdims=True)
        hit = jnp.logical_and(s == mx, mx > -jnp.inf)
        idx = jnp.min(jnp.where(hit, blk, scores.shape[axis]), axis=axis, keepdims=True)
        pick = blk == idx
        bias = jnp.where(pick, 0.0, bias)
        s = jnp.where(pick, -jnp.inf, s)
    return bias


def _moba_prompt_kernel(qt_ref, kt_ref, vt_ref, o_ref, k_ref, vtb_ref, means_ref, bias_ref, s_ref, p_ref,
                        *, nb, group):
    qi = pl.program_id(2)
    blk = MOBA_BLOCK
    tq = qt_ref.shape[1]
    extents = list(range(group, nb, group)) + [nb]

    def for_extent(fn):
        lo = 0
        for nk in extents:
            pl.when(jnp.logical_and(qi >= lo, qi < nk))(functools.partial(fn, nk))
            lo = nk

    @pl.when(qi == 0)
    def _():
        for n in range(nb):
            kn = kt_ref[:, n * blk:(n + 1) * blk].T
            k_ref[n * blk:(n + 1) * blk, :] = kn.astype(BF16)
            means_ref[n:n + 1, :] = jnp.mean(kn, axis=0, keepdims=True)
        vtb_ref[...] = vt_ref[...].astype(BF16)

    qt = qt_ref[...]
    sub = lax.broadcasted_iota(jnp.int32, (LANES, 1), 0)
    blk_row = lax.broadcasted_iota(jnp.int32, (nb, tq), 0)
    ws, biases = [], []
    for h in range(2):
        head = jnp.logical_and(sub >= h * HEAD_DIM, sub < (h + 1) * HEAD_DIM)
        qth = jnp.where(head, qt, 0.0)
        scores = jnp.dot(means_ref[...], qth, preferred_element_type=F32, precision=HIGHEST)
        biases.append(jnp.where(blk_row == qi, 0.0, _top_blocks(scores, qi, 0)))
        ws.append((qth * (HEAD_DIM ** -0.5)).astype(BF16))
    w_all = jnp.concatenate(ws, axis=1)
    bias_ref[...] = jnp.concatenate(biases, axis=1)

    def logits(nk):
        s_ref[0:nk * blk, :] = _dot(k_ref[0:nk * blk, :], w_all)

    for_extent(logits)
    n_used = extents[-1]
    for nk in reversed(extents[:-1]):
        n_used = jnp.where(qi < nk, nk, n_used)

    own = pl.ds(pl.multiple_of(qi * blk, blk), blk)
    key_i = lax.broadcasted_iota(jnp.int32, (blk, 2 * tq), 0)
    q_i = lax.broadcasted_iota(jnp.int32, (blk, 2 * tq), 1) % tq
    s_ref[own, :] = jnp.where(key_i <= q_i, s_ref[own, :], NEG)

    def block_rows(jb):
        return pl.ds(pl.multiple_of(jb * blk, blk), blk)

    def pass_max(jb, m):
        return jnp.maximum(m, jnp.max(s_ref[block_rows(jb), :], axis=0, keepdims=True) + bias_ref[pl.ds(jb, 1), :])

    m = lax.fori_loop(0, n_used, pass_max, jnp.full((1, 2 * tq), NEG, F32))

    def pass_exp(jb, l):
        p = jnp.exp(s_ref[block_rows(jb), :] + (bias_ref[pl.ds(jb, 1), :] - m))
        p_ref[block_rows(jb), :] = p.astype(BF16)
        return l + jnp.sum(p, axis=0, keepdims=True)

    l = lax.fori_loop(0, n_used, pass_exp, jnp.zeros((1, 2 * tq), F32))

    def weighted_values(nk):
        acc = [_dot(vtb_ref[h * HEAD_DIM:(h + 1) * HEAD_DIM, 0:nk * blk], p_ref[0:nk * blk, h * tq:(h + 1) * tq])
               for h in range(2)]
        out_t = jnp.concatenate([acc[0] / l[:, :tq], acc[1] / l[:, tq:]], axis=0)
        o_ref[...] = out_t.T

    for_extent(weighted_values)


def moba_prompt(qt, kt_all, vt_all, layer, m_rows):
    n_batch, _, seq = qt.shape
    nb = seq // MOBA_BLOCK
    tq = MOBA_BLOCK
    n_pairs = ATT_WIDTH // LANES
    kv_spec = pl.BlockSpec((None, None, LANES, seq), lambda b, p, i: (layer, b, p, 0))
    return pl.pallas_call(
        functools.partial(_moba_prompt_kernel, nb=nb, group=4),
        grid=(n_batch, n_pairs, nb),
        in_specs=[pl.BlockSpec((None, LANES, tq), lambda b, p, i: (b, p, i)), kv_spec, kv_spec],
        out_specs=pl.BlockSpec((tq, LANES), lambda b, p, i: (b * nb + i, p)),
        out_shape=jax.ShapeDtypeStruct((m_rows, ATT_WIDTH), F32),
        scratch_shapes=[pltpu.VMEM((seq, LANES), BF16),
                        pltpu.VMEM((LANES, seq), BF16),
                        pltpu.VMEM((nb, LANES), F32),
                        pltpu.VMEM((nb, 2 * tq), F32),
                        pltpu.VMEM((seq, 2 * tq), F32),
                        pltpu.VMEM((seq, 2 * tq), BF16)],
        compiler_params=_params("arbitrary", "arbitrary", "arbitrary"),
        name="moba_prompt",
    )(qt, kt_all, vt_all)


def _moba_sample_kernel(pt_ref, q_ref, ktn_ref, vn_ref, *rest, n_blocks, nbs):
    del pt_ref
    n_pages = 2 * nbs
    k_refs, v_refs = rest[:n_pages], rest[n_pages:2 * n_pages]
    o_ref, qbd_ref, ksum_ref, m_ref, l_ref, oblk_ref = rest[2 * n_pages + 1:]
    step = pl.program_id(1)
    t_new = q_ref.shape[0]
    rows = ATT_HEADS * t_new
    lane = lax.broadcasted_iota(jnp.int32, (1, LANES), 1)

    @pl.when(step == 0)
    def _():
        q = q_ref[...]
        col_head = lax.broadcasted_iota(jnp.int32, (t_new, ATT_WIDTH), 1) // HEAD_DIM
        for h in range(ATT_HEADS):
            qbd_ref[h * t_new:(h + 1) * t_new, :] = jnp.where(col_head == h, q, 0.0)
        ksum_ref[...] = jnp.zeros_like(ksum_ref)
        m_ref[...] = jnp.zeros_like(m_ref)
        l_ref[...] = jnp.zeros_like(l_ref)

    qb = (qbd_ref[...] * (HEAD_DIM ** -0.5)).astype(BF16)
    m_all, l_all, ksum_all = m_ref[...], l_ref[...], ksum_ref[...]
    for j in range(nbs):
        n = step * nbs + j
        kt = jnp.concatenate([k_refs[2 * j][...].reshape(ATT_WIDTH, LANES),
                              k_refs[2 * j + 1][...].reshape(ATT_WIDTH, LANES)], axis=1)
        vt = jnp.concatenate([v_refs[2 * j][...].reshape(ATT_WIDTH, LANES),
                              v_refs[2 * j + 1][...].reshape(ATT_WIDTH, LANES)], axis=1)
        logits = _dot(qb, kt.astype(BF16))
        mx = jnp.max(logits, axis=1, keepdims=True)
        p = jnp.exp(logits - mx)
        oblk_ref[n] = _dot_nt(p.astype(BF16), vt.astype(BF16))
        m_all = jnp.where(lane == n, mx, m_all)
        l_all = jnp.where(lane == n, jnp.sum(p, axis=1, keepdims=True), l_all)
        ksum_all = jnp.where(lane == n, jnp.sum(kt, axis=1, keepdims=True), ksum_all)
    m_ref[...] = m_all
    l_ref[...] = l_all
    ksum_ref[...] = ksum_all

    @pl.when(step == n_blocks // nbs - 1)
    def _():
        qbd = qbd_ref[...]
        scores = jnp.dot(qbd, ksum_all * (1.0 / MOBA_BLOCK), preferred_element_type=F32,
                         precision=HIGHEST)
        bias = _top_blocks(scores, n_blocks, 1)
        lo = _dot(qb, ktn_ref[...].astype(BF16))
        r_q = lax.broadcasted_iota(jnp.int32, (rows, t_new), 0) % t_new
        c_t = lax.broadcasted_iota(jnp.int32, (rows, t_new), 1)
        lo = jnp.where(c_t <= r_q, lo, NEG)
        m_own = jnp.max(lo, axis=1, keepdims=True)
        p_own = jnp.exp(lo - m_own)
        l_own = jnp.sum(p_own, axis=1, keepdims=True)
        o_own = _dot(p_own.astype(BF16), vn_ref[...].astype(BF16))
        m_sel = m_all + bias
        m_tot = jnp.maximum(jnp.max(m_sel, axis=1, keepdims=True), m_own)
        w = jnp.exp(m_sel - m_tot)
        w_own = jnp.exp(m_own - m_tot)
        l_tot = jnp.sum(w * l_all, axis=1, keepdims=True) + w_own * l_own
        o_tot = w_own * o_own
        for j in range(n_blocks):
            o_tot = o_tot + w[:, j:j + 1] * oblk_ref[j]
        o_tot = o_tot / l_tot
        col_head = lax.broadcasted_iota(jnp.int32, (t_new, ATT_WIDTH), 1) // HEAD_DIM
        out = jnp.zeros((t_new, ATT_WIDTH), F32)
        for h in range(ATT_HEADS):
            out = jnp.where(col_head == h, o_tot[h * t_new:(h + 1) * t_new, :], out)
        o_ref[...] = out


def moba_sample(qkv_s, ktn, cache_kt, cache_vt, page_table, layer, t_new, att, row0):
    n_seq, n_pages = page_table.shape
    assert MOBA_BLOCK == 2 * cache_kt.shape[-1]
    n_blocks = n_pages // 2
    nbs = 4 if n_blocks % 4 == 0 else 1
    rows = ATT_HEADS * t_new
    page_shape = (None, None) + cache_kt.shape[2:]

    def page_spec(j):
        return pl.BlockSpec(page_shape, lambda b, s, pt: (layer, pt[b, 2 * nbs * s + j], 0, 0, 0))

    pages = [page_spec(j) for j in range(2 * nbs)]
    grid_spec = pltpu.PrefetchScalarGridSpec(
        num_scalar_prefetch=1,
        grid=(n_seq, n_blocks // nbs),
        in_specs=[pl.BlockSpec((t_new, ATT_WIDTH), lambda b, s, pt: (b, 0)),
                  pl.BlockSpec((None, ATT_WIDTH, t_new), lambda b, s, pt: (b, 0, 0)),
                  pl.BlockSpec((t_new, ATT_WIDTH), lambda b, s, pt: (b, 2))] + pages + pages
                 + [pl.BlockSpec(memory_space=pl.ANY)],
        out_specs=pl.BlockSpec((t_new, ATT_WIDTH), lambda b, s, pt: (row0 // t_new + b, 0)),
        scratch_shapes=[pltpu.VMEM((rows, ATT_WIDTH), F32),
                        pltpu.VMEM((ATT_WIDTH, LANES), F32),
                        pltpu.VMEM((rows, LANES), F32), pltpu.VMEM((rows, LANES), F32),
                        pltpu.VMEM((n_blocks, rows, ATT_WIDTH), F32)],
    )
    return pl.pallas_call(
        functools.partial(_moba_sample_kernel, n_blocks=n_blocks, nbs=nbs),
        grid_spec=grid_spec,
        out_shape=jax.ShapeDtypeStruct(att.shape, F32),
        input_output_aliases={4 + 4 * nbs: 0},
        compiler_params=_params("arbitrary", "arbitrary"),
        name="moba_sample",
    )(page_table, qkv_s, ktn, qkv_s, *([cache_kt] * (2 * nbs)), *([cache_vt] * (2 * nbs)), att)


def _shift_rows(x, d, fill):
    row = lax.broadcasted_iota(jnp.int32, x.shape, 0)
    return jnp.where(row >= d, pltpu.roll(x, d, 0), fill)


def _lru_kernel(u_ref, gate_ref, cbuf_ref, h0_ref, cw_ref, cb_ref, wa_ref, ba_ref, wx_ref, bx_ref, lam_ref,
                *rest, tc, n_chunks):
    rec_ref, hl_ref, xbuf, hc = rest[-4:]
    c = pl.program_id(1)

    @pl.when(c == 0)
    def _():
        xbuf[0:SUBLANES, :] = cbuf_ref[...]
        hc[...] = h0_ref[...]

    xbuf[SUBLANES:SUBLANES + tc, :] = u_ref[...]
    w = cw_ref[...]
    uc = cb_ref[...]
    for t in range(CONV_W):
        off = SUBLANES - (CONV_W - 1) + t
        uc = uc + xbuf[off:off + tc, :] * w[t:t + 1, :]
    xbuf[0:SUBLANES, :] = xbuf[tc:tc + SUBLANES, :]

    ub = uc.astype(BF16)
    gate_r = _sigmoid(_dot(ub, wa_ref[...]) + ba_ref[...])
    gate_i = _sigmoid(_dot(ub, wx_ref[...]) + bx_ref[...])
    log_a = -LRU_C * gate_r * _softplus(-lam_ref[...])
    a = jnp.exp(log_a)
    bterm = jnp.sqrt(-jnp.tanh(log_a) * (a * a + 1.0)) * (gate_i * uc)
    row = lax.broadcasted_iota(jnp.int32, a.shape, 0)
    bterm = bterm + jnp.where(row == 0, a * hc[...], 0.0)
    d = 1
    while d < tc:
        b_sh = _shift_rows(bterm, d, 0.0)
        a_sh = _shift_rows(a, d, 1.0)
        bterm = a * b_sh + bterm
        a = a * a_sh
        d *= 2
    h = bterm
    hc[...] = h[tc - 1:tc, :]
    rec_ref[...] = h * _gelu_tanh(gate_ref[...])

    @pl.when(c == n_chunks - 1)
    def _():
        hl_ref[...] = h[tc - 1:tc, :]


def lru_mixer(proj, row0, n_batch, seq, tc, cbuf, h0, cw, cb, wa, ba, wx, bx, lam, rec=None):
    n_chunks = seq // tc
    blk0 = row0 // tc
    carried = [] if rec is None else [rec]
    vec = lambda: pl.BlockSpec((1, LRU_WIDTH), lambda b, c: (0, 0))
    return pl.pallas_call(
        functools.partial(_lru_kernel, tc=tc, n_chunks=n_chunks),
        grid=(n_batch, n_chunks),
        in_specs=[pl.BlockSpec((tc, LRU_WIDTH), lambda b, c: (blk0 + b * n_chunks + c, 0)),
                  pl.BlockSpec((tc, LRU_WIDTH), lambda b, c: (blk0 + b * n_chunks + c, 1)),
                  pl.BlockSpec((None, SUBLANES, LRU_WIDTH), lambda b, c: (b, 0, 0)),
                  pl.BlockSpec((None, 1, LRU_WIDTH), lambda b, c: (b, 0, 0)),
                  pl.BlockSpec((CONV_W, LRU_WIDTH), lambda b, c: (0, 0)),
                  vec(),
                  pl.BlockSpec((LRU_WIDTH, LRU_WIDTH), lambda b, c: (0, 0)),
                  vec(),
                  pl.BlockSpec((LRU_WIDTH, LRU_WIDTH), lambda b, c: (0, 0)),
                  vec(), vec()] + [pl.BlockSpec(memory_space=pl.ANY)] * len(carried),
        out_specs=[pl.BlockSpec((tc, LRU_WIDTH), lambda b, c: (blk0 + b * n_chunks + c, 0)),
                   pl.BlockSpec((None, 1, LRU_WIDTH), lambda b, c: (b, 0, 0))],
        out_shape=[jax.ShapeDtypeStruct((proj.shape[0], LRU_WIDTH), F32),
                   jax.ShapeDtypeStruct((n_batch, 1, LRU_WIDTH), F32)],
        input_output_aliases={11: 0} if carried else {},
        scratch_shapes=[pltpu.VMEM((SUBLANES + tc, LRU_WIDTH), F32), pltpu.VMEM((1, LRU_WIDTH), F32)],
        compiler_params=_params("arbitrary", "arbitrary"),
        name="lru_mixer",
    )(proj, proj, cbuf, h0, cw, cb, wa, ba, wx, bx, lam, *carried)


def _expand_heads(v, g):
    lane = lax.broadcasted_iota(jnp.int32, (1, SSD_GROUP_W), 1)
    out = v[:, SSD_HPG * g + SSD_HPG - 1:SSD_HPG * g + SSD_HPG]
    for k in range(SSD_HPG - 2, -1, -1):
        out = jnp.where(lane < (k + 1) * SSD_HEAD_DIM, v[:, SSD_HPG * g + k:SSD_HPG * g + k + 1], out)
    return out


def _ssd_kernel(x_ref, zx_ref, cbuf_ref, h0_ref, wdt_ref, wdtt_ref, dtb_ref, dtbt_ref, alog_ref, alogt_ref,
                cw_ref, cb_ref, dskip_ref, ng_ref, *rest, tc, n_chunks):
    y_ref, hl_ref, xbuf, st_ref = rest[-4:]
    c = pl.program_id(1)

    @pl.when(c == 0)
    def _():
        xbuf[0:SUBLANES, :] = cbuf_ref[...]
        for g in range(SSD_GROUPS):
            st_ref[g] = h0_ref[g * SSD_GROUP_W:(g + 1) * SSD_GROUP_W, :].T

    xbuf[SUBLANES:SUBLANES + tc, :] = zx_ref[:, SSD_INNER:]
    cw = cw_ref[...]
    cb = cb_ref[...]

    def conv_silu(lo, width):
        acc = cb[:, lo:lo + width]
        for t in range(CONV_W):
            off = SUBLANES - (CONV_W - 1) + t
            acc = acc + xbuf[off:off + tc, lo:lo + width] * cw[t:t + 1, lo:lo + width]
        return _silu(acc)

    xb = x_ref[...].astype(BF16)
    dt = _softplus(_dot(xb, wdt_ref[...]) + dtb_ref[...])
    dtt = _softplus(_dot_nt(wdtt_ref[...], xb) + dtbt_ref[...])
    a_neg = -jnp.exp(alog_ref[...])
    a_negt = -jnp.exp(alogt_ref[...])
    r = lax.broadcasted_iota(jnp.int32, (tc, tc), 0)
    s = lax.broadcasted_iota(jnp.int32, (tc, tc), 1)
    causal = s <= r
    tri = causal.astype(F32)
    cs = jnp.dot(tri, dt * a_neg, preferred_element_type=F32, precision=HIGHEST)
    cst = _dot_nt(dtt * a_negt, tri, precision=HIGHEST)
    cs_end = cs[tc - 1:tc, :]
    w_end = jnp.exp(cs_end - cs) * dt
    ecs = jnp.exp(cs)
    chunk_decay = jnp.exp(cs_end)
    lane = lax.broadcasted_iota(jnp.int32, (1, SSD_GROUP_W), 1)

    for g in range(SSD_GROUPS):
        xg = conv_silu(g * SSD_GROUP_W, SSD_GROUP_W)
        bg = conv_silu(SSD_INNER + g * SSD_STATE, SSD_STATE).astype(BF16)
        cg = conv_silu(SSD_INNER + SSD_GROUPS * SSD_STATE + g * SSD_STATE, SSD_STATE).astype(BF16)
        cbm = _dot_nt(cg, bg)
        y = jnp.zeros((tc, SSD_GROUP_W), F32)
        for k in range(SSD_HPG):
            e = SSD_HPG * g + k
            dec = jnp.exp(jnp.where(causal, cs[:, e:e + 1] - cst[e:e + 1, :], -jnp.inf))
            m = (cbm * dec * dtt[e:e + 1, :]).astype(BF16)
            head = jnp.logical_and(lane >= k * SSD_HEAD_DIM, lane < (k + 1) * SSD_HEAD_DIM)
            y = y + _dot(m, jnp.where(head, xg, 0.0).astype(BF16))
        st = st_ref[g]
        y = y + _dot(cg, st.astype(BF16)) * _expand_heads(ecs, g)
        xw = (xg * _expand_heads(w_end, g)).astype(BF16)
        st_ref[g] = st * _expand_heads(chunk_decay, g) + _dot_tn(bg, xw)
        y = y + dskip_ref[:, g * SSD_GROUP_W:(g + 1) * SSD_GROUP_W] * xg
        y = y * _silu(zx_ref[:, g * SSD_GROUP_W:(g + 1) * SSD_GROUP_W])
        y = y * lax.rsqrt(jnp.mean(y * y, axis=-1, keepdims=True) + RMS_EPS)
        y_ref[:, g * SSD_GROUP_W:(g + 1) * SSD_GROUP_W] = y * ng_ref[:, g * SSD_GROUP_W:(g + 1) * SSD_GROUP_W]

    xbuf[0:SUBLANES, :] = xbuf[tc:tc + SUBLANES, :]

    @pl.when(c == n_chunks - 1)
    def _():
        for g in range(SSD_GROUPS):
            hl_ref[g * SSD_GROUP_W:(g + 1) * SSD_GROUP_W, :] = st_ref[g].T


def ssd_mixer(x, zx, row0, n_batch, seq, tc, cbuf, h0, layer, wdt, wdtt, dtb, alog, cw, cb, dskip, ng, y=None):
    n_chunks = seq // tc
    blk0 = row0 // tc
    m_state = SSD_HEADS * SSD_HEAD_DIM
    carried = [] if y is None else [y]
    const2 = lambda shape: pl.BlockSpec(shape, lambda b, c: (0, 0))
    return pl.pallas_call(
        functools.partial(_ssd_kernel, tc=tc, n_chunks=n_chunks),
        grid=(n_batch, n_chunks),
        in_specs=[pl.BlockSpec((tc, D_MODEL), lambda b, c: (blk0 + b * n_chunks + c, 0)),
                  pl.BlockSpec((tc, SSD_INNER + SSD_CONV_DIM), lambda b, c: (blk0 + b * n_chunks + c, 0)),
                  pl.BlockSpec((None, SUBLANES, SSD_CONV_DIM), lambda b, c: (b, 0, 0)),
                  pl.BlockSpec((None, None, m_state, SSD_STATE), lambda b, c: (layer, b, 0, 0)),
                  const2((D_MODEL, LANES)), const2((LANES, D_MODEL)),
                  const2((1, LANES)), const2((LANES, 1)), const2((1, LANES)), const2((LANES, 1)),
                  const2((CONV_W, SSD_CONV_DIM)), const2((1, SSD_CONV_DIM)),
                  const2((1, SSD_INNER)), const2((1, SSD_INNER))] + [pl.BlockSpec(memory_space=pl.ANY)] * len(carried),
        out_specs=[pl.BlockSpec((tc, SSD_INNER), lambda b, c: (blk0 + b * n_chunks + c, 0)),
                   pl.BlockSpec((None, m_state, SSD_STATE), lambda b, c: (b, 0, 0))],
        out_shape=[jax.ShapeDtypeStruct((x.shape[0], SSD_INNER), F32),
                   jax.ShapeDtypeStruct((n_batch, m_state, SSD_STATE), F32)],
        input_output_aliases={14: 0} if carried else {},
        scratch_shapes=[pltpu.VMEM((SUBLANES + tc, SSD_CONV_DIM), F32),
                        pltpu.VMEM((SSD_GROUPS, SSD_STATE, SSD_GROUP_W), F32)],
        compiler_params=_params("arbitrary", "arbitrary"),
        name="ssd_mixer",
    )(x, zx, cbuf, h0, wdt, wdtt, dtb.reshape(1, LANES), dtb.reshape(LANES, 1),
      alog.reshape(1, LANES), alog.reshape(LANES, 1), cw, cb, dskip, ng, *carried)


def _block_diag(w):
    n, d, e = w.shape
    eye = jnp.eye(n, dtype=w.dtype)
    return (eye[:, None, :, None] * w[:, :, None, :]).reshape(n * d, n * e)


def _conv_tail(buf):
    return jnp.pad(buf, ((0, 0), (SUBLANES - (CONV_W - 1), 0), (0, 0)))


def _last_rows(a, row0, n_batch, seq, col0, width):
    rows = [lax.slice(a, (row0 + b * seq + seq - (CONV_W - 1), col0), (row0 + (b + 1) * seq, col0 + width))
            for b in range(n_batch)]
    return jnp.stack(rows)


def kernel(x_prompt, x_sample, cache_k, cache_v, page_table, state_lru_h, state_lru_conv, state_ssm, state_ssd_conv, w_in_even, lru_conv_w, lru_conv_b, lru_w_a, lru_b_a, lru_w_x, lru_b_x, lru_lambda, w_out_even, ssd_w_in, ssd_conv_w, ssd_conv_b, ssd_dt_bias, ssd_a_log, ssd_d, ssd_norm_g, ssd_w_out, ffn_w1, ffn_w3, ffn_w2, moe_router, moe_w1, moe_w3, moe_w2, ln_mix_g, ln_mix_b, ln_ffn_g, ln_ffn_b):
    bp, sp, d = x_prompt.shape
    bs, ts, _ = x_sample.shape
    mp = bp * sp
    ms = bs * ts
    m = mp + ms
    tm = 640 if m % 640 == 0 else ts * 8
    tm_ffn = 832 if m % 832 == 0 else tm
    n_even, n_odd = w_in_even.shape[0], ssd_w_in.shape[0]
    x = jnp.concatenate([x_prompt.reshape(mp, d), x_sample.reshape(ms, d)], axis=0)

    cache_kt = cache_k.transpose(0, 1, 3, 4, 2)
    cache_vt = cache_v.transpose(0, 1, 3, 4, 2)
    ffn_w = (ffn_w1.astype(BF16), ffn_w3.astype(BF16), ffn_w2.astype(BF16))
    moe_w = (moe_w1.astype(BF16), moe_w3.astype(BF16), moe_w2.astype(BF16))
    m_state = SSD_HEADS * SSD_HEAD_DIM
    ssm_in = state_ssm.reshape(n_odd, bs, m_state, SSD_STATE)

    ks_s, vs_s = [], []
    lruh_p, lruh_s, lrucv_p, lrucv_s = [], [], [], []
    ssm_p, ssm_s, ssdcv_p, ssdcv_s = [], [], [], []
    zeros_lru_cbuf = jnp.zeros((bp, SUBLANES, LRU_WIDTH), F32)
    zeros_lru_h = jnp.zeros((bp, 1, LRU_WIDTH), F32)
    zeros_ssd_cbuf = jnp.zeros((bp, SUBLANES, SSD_CONV_DIM), F32)
    zeros_ssm = jnp.zeros((1, bp, m_state, SSD_STATE), F32)
    no_gates = jnp.zeros((m, LANES), F32)
    kv_all = None

    for layer in range(DEPTH):
        i = layer // 2
        if layer % 2 == 0:
            n_qkv = 3 * ATT_WIDTH
            w_in = w_in_even[i].astype(BF16)
            w_qkv = w_in[:, :n_qkv]
            qt, kt_all, vt_all = qkv_transposed(w_qkv.T, x, bp, sp, 512 if sp % 512 == 0 else sp, i, n_even, kv_all)
            kv_all = (kt_all, vt_all)
            proj = matmul(x, w_in[:, n_qkv:], 2 * LRU_WIDTH, tm, 2 * LRU_WIDTH)
            qkv_s = matmul(x[mp:], w_qkv, n_qkv, ms, n_qkv)
            k_s = qkv_s[:, ATT_WIDTH:2 * ATT_WIDTH]
            v_s = qkv_s[:, 2 * ATT_WIDTH:]
            ktn = k_s.reshape(bs, ts, ATT_WIDTH).transpose(0, 2, 1)
            att = moba_prompt(qt, kt_all, vt_all, i, m)
            att = moba_sample(qkv_s, ktn, cache_kt, cache_vt, page_table, i, ts, att, mp)

            wa = _block_diag(lru_w_a[i]).astype(BF16)
            wx = _block_diag(lru_w_x[i]).astype(BF16)
            lru_args = (lru_conv_w[i], lru_conv_b[i].reshape(1, -1), wa, lru_b_a[i].reshape(1, -1),
                        wx, lru_b_x[i].reshape(1, -1), lru_lambda[i].reshape(1, -1))
            tc_p = 256 if sp % 256 == 0 else sp
            rec, hl_p = lru_mixer(proj, 0, bp, sp, tc_p, zeros_lru_cbuf, zeros_lru_h, *lru_args)
            rec, hl_s = lru_mixer(proj, mp, bs, ts, ts, _conv_tail(state_lru_conv[i]),
                                  state_lru_h[i].reshape(bs, 1, LRU_WIDTH), *lru_args, rec=rec)
            w_out = w_out_even[i].astype(BF16)
            x = matmul_residual_ln([att, rec], [w_out[:ATT_WIDTH], w_out[ATT_WIDTH:]], x,
                                   ln_mix_g[layer], ln_mix_b[layer], tm)
            x = swiglu_residual_ln(x, no_gates, *ffn_w, i, ln_ffn_g[layer], ln_ffn_b[layer], tm_ffn, 2, False)

            ks_s.append(k_s.reshape(bs, ts, ATT_HEADS, HEAD_DIM))
            vs_s.append(v_s.reshape(bs, ts, ATT_HEADS, HEAD_DIM))
            lruh_p.append(hl_p.reshape(bp, LRU_WIDTH))
            lruh_s.append(hl_s.reshape(bs, LRU_WIDTH))
            lrucv_p.append(_last_rows(proj, 0, bp, sp, 0, LRU_WIDTH))
            lrucv_s.append(_last_rows(proj, mp, bs, ts, 0, LRU_WIDTH))
        else:
            w_in = ssd_w_in[i]
            n_zx = SSD_INNER + SSD_CONV_DIM
            zx = matmul(x, w_in[:, :n_zx].astype(BF16), n_zx, tm, 2048)
            wdt = jnp.pad(w_in[:, n_zx:], ((0, 0), (0, LANES - SSD_HEADS))).astype(BF16)
            pad_h = lambda v: jnp.pad(v, (0, LANES - SSD_HEADS))
            dskip = jnp.repeat(ssd_d[i], SSD_HEAD_DIM).reshape(1, SSD_INNER)
            ssd_args = (wdt, wdt.T, pad_h(ssd_dt_bias[i]), pad_h(ssd_a_log[i]), ssd_conv_w[i],
                        ssd_conv_b[i].reshape(1, -1), dskip, ssd_norm_g[i].reshape(1, -1))
            tc_p = math.gcd(sp, SSD_CHUNK)
            y, st_p = ssd_mixer(x, zx, 0, bp, sp, tc_p, zeros_ssd_cbuf, zeros_ssm, 0, *ssd_args)
            y, st_s = ssd_mixer(x, zx, mp, bs, ts, ts, _conv_tail(state_ssd_conv[i]), ssm_in, i, *ssd_args, y=y)
            router_t = jnp.pad(moe_router[i].T, ((0, LANES - N_EXPERTS), (0, 0)))
            x, gates = matmul_residual_ln([y], [ssd_w_out[i].astype(BF16)], x, ln_mix_g[layer], ln_mix_b[layer], tm,
                                          router_t=router_t)
            x = swiglu_residual_ln(x, gates, *moe_w, i, ln_ffn_g[layer], ln_ffn_b[layer], tm_ffn, N_EXPERTS, True)

            ssm_p.append(st_p.reshape(bp, SSD_HEADS, SSD_HEAD_DIM, SSD_STATE))
            ssm_s.append(st_s.reshape(bs, SSD_HEADS, SSD_HEAD_DIM, SSD_STATE))
            ssdcv_p.append(_last_rows(zx, 0, bp, sp, SSD_INNER, SSD_CONV_DIM))
            ssdcv_s.append(_last_rows(zx, mp, bs, ts, SSD_INNER, SSD_CONV_DIM))

    head_major = lambda a: a.reshape(n_even, bp, ATT_HEADS, HEAD_DIM, sp).transpose(0, 1, 4, 2, 3)
    return (x[:mp].reshape(bp, sp, d), x[mp:].reshape(bs, ts, d),
            head_major(kv_all[0]), head_major(kv_all[1]), jnp.stack(ks_s), jnp.stack(vs_s),
            jnp.stack(lruh_p), jnp.stack(lruh_s), jnp.stack(lrucv_p), jnp.stack(lrucv_s),
            jnp.stack(ssm_p), jnp.stack(ssm_s), jnp.stack(ssdcv_p), jnp.stack(ssdcv_s))
```

```python
import functools
import math

import jax
import jax.numpy as jnp
from jax import lax
from jax.experimental import pallas as pl
from jax.experimental.pallas import tpu as pltpu

F32 = jnp.float32
BF16 = jnp.bfloat16

D_MODEL = 1024
DEPTH = 4
ATT_HEADS = 8
HEAD_DIM = 64
ATT_WIDTH = ATT_HEADS * HEAD_DIM
MOBA_BLOCK = 256
MOBA_TOPK = 3
LRU_WIDTH = 512
LRU_BLOCKS = 8
LRU_C = 8.0
CONV_W = 4
SSD_INNER = 2048
SSD_HEAD_DIM = 64
SSD_HEADS = 32
SSD_GROUPS = 8
SSD_HPG = 4
SSD_STATE = 128
SSD_CHUNK = 128
SSD_GROUP_W = SSD_HPG * SSD_HEAD_DIM
SSD_CONV_DIM = SSD_INNER + 2 * SSD_GROUPS * SSD_STATE
N_EXPERTS = 8
DN_ALPHA = (2 * DEPTH) ** 0.25
LN_EPS = 1e-5
RMS_EPS = 1e-6

LANES = 128
SUBLANES = 8
VMEM_LIMIT = 56 * 1024 * 1024
NEG = -1e30
LOG2E = 1.4426950408889634
ONES_ROWS = 16
HIGHEST = lax.Precision.HIGHEST


def _params(*sem):
    return pltpu.CompilerParams(dimension_semantics=sem, vmem_limit_bytes=VMEM_LIMIT)


def _dot(a, b):
    return jnp.dot(a, b, preferred_element_type=F32)


def _dot_nt(a, b, precision=None):
    return lax.dot_general(a, b, (((1,), (1,)), ((), ())), preferred_element_type=F32, precision=precision)


def _dot_tn(a, b):
    return lax.dot_general(a, b, (((0,), (0,)), ((), ())), preferred_element_type=F32)


def _sigmoid(x):
    return 0.5 * jnp.tanh(0.5 * x) + 0.5


def _silu(x):
    return x * _sigmoid(x)


def _softplus(x):
    return jnp.maximum(x, 0.0) + jnp.log1p(jnp.exp(-jnp.abs(x)))


def _gelu_tanh(x):
    return 0.5 * x * (1.0 + jnp.tanh(math.sqrt(2.0 / math.pi) * (x + 0.044715 * (x * x * x))))


def _layer_norm(y, g, b):
    mu = jnp.mean(y, axis=-1, keepdims=True)
    d = y - mu
    var = jnp.mean(d * d, axis=-1, keepdims=True)
    return d * lax.rsqrt(var + LN_EPS) * g + b


def _mm_kernel(x_ref, w_ref, o_ref):
    o_ref[...] = _dot(x_ref[...].astype(BF16), w_ref[...])


def matmul(x, w, n_cols, tm, tn):
    m, k = x.shape
    return pl.pallas_call(
        _mm_kernel,
        grid=(n_cols // tn, m // tm),
        in_specs=[pl.BlockSpec((tm, k), lambda j, i: (i, 0)),
                  pl.BlockSpec((k, tn), lambda j, i: (0, j))],
        out_specs=pl.BlockSpec((tm, tn), lambda j, i: (i, j)),
        out_shape=jax.ShapeDtypeStruct((m, n_cols), F32),
        compiler_params=_params("arbitrary", "arbitrary"),
        name="matmul",
    )(x, w)


def _qkv_t_kernel(w_ref, x_ref, *rest):
    q_ref, k_ref, v_ref = rest[-3:]
    res = _dot_nt(w_ref[...], x_ref[...].astype(BF16))
    q_ref[...] = res[:ATT_WIDTH]
    k_ref[...] = res[ATT_WIDTH:2 * ATT_WIDTH]
    v_ref[...] = res[2 * ATT_WIDTH:]


def qkv_transposed(w_t, x, n_batch, seq, ts, layer, n_layers, kv_prev):
    n, k = w_t.shape
    nblk = seq // ts
    kv_shape = jax.ShapeDtypeStruct((n_layers, n_batch, ATT_WIDTH, seq), F32)
    kv_spec = pl.BlockSpec((None, None, ATT_WIDTH, ts), lambda b, s: (layer, b, 0, s))
    carried = [] if kv_prev is None else list(kv_prev)
    return pl.pallas_call(
        _qkv_t_kernel,
        grid=(n_batch, nblk),
        in_specs=[pl.BlockSpec((n, k), lambda b, s: (0, 0)),
                  pl.BlockSpec((ts, k), lambda b, s: (b * nblk + s, 0))]
                 + [pl.BlockSpec(memory_space=pl.ANY)] * len(carried),
        out_specs=[pl.BlockSpec((None, ATT_WIDTH, ts), lambda b, s: (b, 0, s)), kv_spec, kv_spec],
        out_shape=[jax.ShapeDtypeStruct((n_batch, ATT_WIDTH, seq), F32), kv_shape, kv_shape],
        input_output_aliases={2 + j: 1 + j for j in range(len(carried))},
        compiler_params=_params("arbitrary", "arbitrary"),
        name="qkv_transposed",
    )(w_t, x, *carried)


def _top2_gates(logits):
    lane = lax.broadcasted_iota(jnp.int32, logits.shape, 1)
    logits = jnp.where(lane < N_EXPERTS, logits, -jnp.inf)
    m1 = jnp.max(logits, axis=1, keepdims=True)
    i1 = jnp.min(jnp.where(logits == m1, lane, LANES), axis=1, keepdims=True)
    rest = jnp.where(lane == i1, -jnp.inf, logits)
    m2 = jnp.max(rest, axis=1, keepdims=True)
    i2 = jnp.min(jnp.where(rest == m2, lane, LANES), axis=1, keepdims=True)
    e2 = jnp.exp(m2 - m1)
    g1 = 1.0 / (1.0 + e2)
    g2 = e2 * g1
    return jnp.where(lane == i1, g1, 0.0) + jnp.where(lane == i2, g2, 0.0)


def _mm_ln_kernel(*refs, n_pairs, routed):
    a_refs = refs[:n_pairs]
    w_refs = refs[n_pairs:2 * n_pairs]
    r_ref, g_ref, b_ref = refs[2 * n_pairs:2 * n_pairs + 3]
    mix = _dot(a_refs[0][...].astype(BF16), w_refs[0][...])
    for a_ref, w_ref in zip(a_refs[1:], w_refs[1:]):
        mix = mix + _dot(a_ref[...].astype(BF16), w_ref[...])
    y = _layer_norm(DN_ALPHA * r_ref[...] + mix, g_ref[...], b_ref[...])
    if routed:
        rhi_ref, rlo_ref, o_ref, gates_ref = refs[2 * n_pairs + 3:]
        y_hi = y.astype(BF16)
        y_lo = (y - y_hi.astype(F32)).astype(BF16)
        logits = _dot_nt(y_hi, rhi_ref[...]) + (_dot_nt(y_lo, rhi_ref[...]) + _dot_nt(y_hi, rlo_ref[...]))
        gates_ref[...] = _top2_gates(logits)
    else:
        o_ref, = refs[2 * n_pairs + 3:]
    o_ref[...] = y


def matmul_residual_ln(acts, weights, resid, g, b, tm, router_t=None):
    m, d = resid.shape
    n_pairs = len(acts)
    routed = router_t is not None
    in_specs = [pl.BlockSpec((tm, a.shape[1]), lambda i: (i, 0)) for a in acts]
    in_specs += [pl.BlockSpec(w.shape, lambda i: (0, 0)) for w in weights]
    in_specs += [pl.BlockSpec((tm, d), lambda i: (i, 0)),
                 pl.BlockSpec((1, d), lambda i: (0, 0)),
                 pl.BlockSpec((1, d), lambda i: (0, 0))]
    out_specs = [pl.BlockSpec((tm, d), lambda i: (i, 0))]
    out_shape = [jax.ShapeDtypeStruct((m, d), F32)]
    extra = []
    if routed:
        in_specs += [pl.BlockSpec((LANES, d), lambda i: (0, 0))] * 2
        out_specs.append(pl.BlockSpec((tm, LANES), lambda i: (i, 0)))
        out_shape.append(jax.ShapeDtypeStruct((m, LANES), F32))
        r_hi = router_t.astype(BF16)
        extra = [r_hi, (router_t - r_hi.astype(F32)).astype(BF16)]
    out = pl.pallas_call(
        functools.partial(_mm_ln_kernel, n_pairs=n_pairs, routed=routed),
        grid=(m // tm,),
        in_specs=in_specs,
        out_specs=out_specs,
        out_shape=out_shape,
        compiler_params=_params("arbitrary"),
        name="matmul_residual_ln",
    )(*acts, *weights, resid, g.reshape(1, d), b.reshape(1, d), *extra)
    return out if routed else out[0]


def _ffn_kernel(x_ref, gate_ref, w1_ref, w3_ref, w2_ref, g_ref, b_ref, o_ref, xb_ref, acc_ref, *, n_e, gated):
    e = pl.program_id(1)

    @pl.when(e == 0)
    def _():
        xb_ref[...] = x_ref[...].astype(BF16)
        acc_ref[...] = jnp.zeros_like(acc_ref)

    xb = xb_ref[...]
    h = _silu(_dot(xb, w1_ref[...])) * _dot(xb, w3_ref[...])
    if gated:
        lane = lax.broadcasted_iota(jnp.int32, (1, LANES), 1)
        ge = jnp.sum(jnp.where(lane == e, gate_ref[...], 0.0), axis=1, keepdims=True)
        h = h * ge
    acc_ref[...] += _dot(h.astype(BF16), w2_ref[...])

    @pl.when(e == n_e - 1)
    def _():
        o_ref[...] = _layer_norm(DN_ALPHA * x_ref[...] + acc_ref[...], g_ref[...], b_ref[...])


def swiglu_residual_ln(x, gates, w1, w3, w2, layer, g, b, tm, n_e, gated):
    m, d = x.shape
    if gated:
        f = w1.shape[3]
        w13_spec = pl.BlockSpec((None, None, d, f), lambda i, e: (layer, e, 0, 0))
        w2_spec = pl.BlockSpec((None, None, f, d), lambda i, e: (layer, e, 0, 0))
    else:
        f = w1.shape[2] // n_e
        w13_spec = pl.BlockSpec((None, d, f), lambda i, e: (layer, 0, e))
        w2_spec = pl.BlockSpec((None, f, d), lambda i, e: (layer, e, 0))
    return pl.pallas_call(
        functools.partial(_ffn_kernel, n_e=n_e, gated=gated),
        grid=(m // tm, n_e),
        in_specs=[pl.BlockSpec((tm, d), lambda i, e: (i, 0)),
                  pl.BlockSpec((tm, LANES), lambda i, e: (i, 0)),
                  w13_spec, w13_spec, w2_spec,
                  pl.BlockSpec((1, d), lambda i, e: (0, 0)),
                  pl.BlockSpec((1, d), lambda i, e: (0, 0))],
        out_specs=pl.BlockSpec((tm, d), lambda i, e: (i, 0)),
        out_shape=jax.ShapeDtypeStruct((m, d), F32),
        scratch_shapes=[pltpu.VMEM((tm, d), BF16), pltpu.VMEM((tm, d), F32)],
        compiler_params=_params("arbitrary", "arbitrary"),
        name="swiglu_residual_ln",
    )(x, gates, w1, w3, w2, g.reshape(1, d), b.reshape(1, d))


def _top_blocks(scores, n_valid, axis):
    blk = lax.broadcasted_iota(jnp.int32, scores.shape, axis)
    s = jnp.where(blk < n_valid, scores, -jnp.inf)
    bias = jnp.full(scores.shape, NEG, F32)
    for _ in range(MOBA_TOPK):
        mx = jnp.max(s, axis=axis, keepdims=True)
        hit = jnp.logical_and(s == mx, mx > -jnp.inf)
        idx = jnp.min(jnp.where(hit, blk, scores.shape[axis]), axis=axis, keepdims=True)
        pick = blk == idx
        bias = jnp.where(pick, 0.0, bias)
        s = jnp.where(pick, -jnp.inf, s)
    return bias


def _moba_prompt_kernel(qt_ref, kt_ref, vt_ref, o_ref, k_ref, vte_ref, means_ref, bias_ref, mblk_ref, s_ref, p_ref,
                        *, nb, group):
    qi = pl.program_id(2)
    blk = MOBA_BLOCK
    tq = qt_ref.shape[1]
    extents = list(range(group, nb, group)) + [nb]

    def for_extent(fn):
        lo = 0
        for nk in extents:
            pl.when(jnp.logical_and(qi >= lo, qi < nk))(functools.partial(fn, nk))
            lo = nk

    @pl.when(qi == 0)
    def _():
        for n in range(nb):
            kn = kt_ref[:, n * blk:(n + 1) * blk].T
            k_ref[n * blk:(n + 1) * blk, :] = kn.astype(BF16)
            means_ref[n:n + 1, :] = jnp.mean(kn, axis=0, keepdims=True)
        for h in range(2):
            vte_ref[h, 0:HEAD_DIM, :] = vt_ref[h * HEAD_DIM:(h + 1) * HEAD_DIM, :].astype(BF16)
            vte_ref[h, HEAD_DIM:, :] = jnp.ones((ONES_ROWS, vt_ref.shape[1]), BF16)
        mblk_ref[...] = jnp.zeros_like(mblk_ref)

    qt = qt_ref[...]
    sub = lax.broadcasted_iota(jnp.int32, (LANES, 1), 0)
    blk_row = lax.broadcasted_iota(jnp.int32, (nb, tq), 0)
    ws, biases = [], []
    for h in range(2):
        head = jnp.logical_and(sub >= h * HEAD_DIM, sub < (h + 1) * HEAD_DIM)
        qth = jnp.where(head, qt, 0.0)
        scores = jnp.dot(means_ref[...], qth, preferred_element_type=F32, precision=HIGHEST)
        biases.append(jnp.where(blk_row == qi, 0.0, _top_blocks(scores, qi, 0)))
        ws.append((qth * (HEAD_DIM ** -0.5 * LOG2E)).astype(BF16))
    w_all = jnp.concatenate(ws, axis=1)
    bias_ref[...] = jnp.concatenate(biases, axis=1)

    def logits(nk):
        s = _dot(k_ref[0:nk * blk, :], w_all)
        s_ref[0:nk * blk, :] = s
        mblk_ref[0:nk, :] = jnp.max(s.reshape(nk, blk, 2 * tq), axis=1)

    for_extent(logits)
    n_used = extents[-1]
    for nk in reversed(extents[:-1]):
        n_used = jnp.where(qi < nk, nk, n_used)

    own = pl.ds(pl.multiple_of(qi * blk, blk), blk)
    key_i = lax.broadcasted_iota(jnp.int32, (blk, 2 * tq), 0)
    q_i = lax.broadcasted_iota(jnp.int32, (blk, 2 * tq), 1) % tq
    s_own = jnp.where(key_i <= q_i, s_ref[own, :], NEG)
    s_ref[own, :] = s_own
    mblk_ref[pl.ds(qi, 1), :] = jnp.max(s_own, axis=0, keepdims=True)
    m = jnp.max(mblk_ref[...] + bias_ref[...], axis=0, keepdims=True)

    def pass_exp(jb, carry):
        rows = pl.ds(pl.multiple_of(jb * blk, blk), blk)
        p_ref[rows, :] = jnp.exp2(s_ref[rows, :] + (bias_ref[pl.ds(jb, 1), :] - m)).astype(BF16)
        return carry

    lax.fori_loop(0, n_used, pass_exp, 0)

    def weighted_values(nk):
        acc = [_dot(vte_ref[h, :, 0:nk * blk], p_ref[0:nk * blk, h * tq:(h + 1) * tq]) for h in range(2)]
        out_t = jnp.concatenate([a[:HEAD_DIM] / a[HEAD_DIM:HEAD_DIM + 1] for a in acc], axis=0)
        o_ref[...] = out_t.T

    for_extent(weighted_values)


def moba_prompt(qt, kt_all, vt_all, layer, m_rows):
    n_batch, _, seq = qt.shape
    nb = seq // MOBA_BLOCK
    tq = MOBA_BLOCK
    n_pairs = ATT_WIDTH // LANES
    kv_spec = pl.BlockSpec((None, None, LANES, seq), lambda b, p, i: (layer, b, p, 0))
    return pl.pallas_call(
        functools.partial(_moba_prompt_kernel, nb=nb, group=4),
        grid=(n_batch, n_pairs, nb),
        in_specs=[pl.BlockSpec((None, LANES, tq), lambda b, p, i: (b, p, i)), kv_spec, kv_spec],
        out_specs=pl.BlockSpec((tq, LANES), lambda b, p, i: (b * nb + i, p)),
        out_shape=jax.ShapeDtypeStruct((m_rows, ATT_WIDTH), F32),
        scratch_shapes=[pltpu.VMEM((seq, LANES), BF16),
                        pltpu.VMEM((2, HEAD_DIM + ONES_ROWS, seq), BF16),
                        pltpu.VMEM((nb, LANES), F32),
                        pltpu.VMEM((nb, 2 * tq), F32),
                        pltpu.VMEM((nb, 2 * tq), F32),
                        pltpu.VMEM((seq, 2 * tq), F32),
                        pltpu.VMEM((seq, 2 * tq), BF16)],
        compiler_params=_params("arbitrary", "arbitrary", "arbitrary"),
        name="moba_prompt",
    )(qt, kt_all, vt_all)


def _moba_sample_kernel(pt_ref, q_ref, ktn_ref, vn_ref, *rest, n_blocks, nbs):
    del pt_ref
    n_pages = 2 * nbs
    k_refs, v_refs = rest[:n_pages], rest[n_pages:2 * n_pages]
    o_ref, qbd_ref, ksum_ref, m_ref, l_ref, oblk_ref = rest[2 * n_pages + 1:]
    step = pl.program_id(1)
    t_new = q_ref.shape[0]
    rows = ATT_HEADS * t_new
    lane = lax.broadcasted_iota(jnp.int32, (1, LANES), 1)

    @pl.when(step == 0)
    def _():
        q = q_ref[...]
        col_head = lax.broadcasted_iota(jnp.int32, (t_new, ATT_WIDTH), 1) // HEAD_DIM
        for h in range(ATT_HEADS):
            qbd_ref[h * t_new:(h + 1) * t_new, :] = jnp.where(col_head == h, q, 0.0)
        ksum_ref[...] = jnp.zeros_like(ksum_ref)
        m_ref[...] = jnp.zeros_like(m_ref)
        l_ref[...] = jnp.zeros_like(l_ref)

    qb = (qbd_ref[...] * (HEAD_DIM ** -0.5)).astype(BF16)
    m_all, l_all, ksum_all = m_ref[...], l_ref[...], ksum_ref[...]
    for j in range(nbs):
        n = step * nbs + j
        kt = jnp.concatenate([k_refs[2 * j][...].reshape(ATT_WIDTH, LANES),
                              k_refs[2 * j + 1][...].reshape(ATT_WIDTH, LANES)], axis=1)
        vt = jnp.concatenate([v_refs[2 * j][...].reshape(ATT_WIDTH, LANES),
                              v_refs[2 * j + 1][...].reshape(ATT_WIDTH, LANES)], axis=1)
        logits = _dot(qb, kt.astype(BF16))
        mx = jnp.max(logits, axis=1, keepdims=True)
        p = jnp.exp(logits - mx)
        oblk_ref[n] = _dot_nt(p.astype(BF16), vt.astype(BF16))
        m_all = jnp.where(lane == n, mx, m_all)
        l_all = jnp.where(lane == n, jnp.sum(p, axis=1, keepdims=True), l_all)
        ksum_all = jnp.where(lane == n, jnp.sum(kt, axis=1, keepdims=True), ksum_all)
    m_ref[...] = m_all
    l_ref[...] = l_all
    ksum_ref[...] = ksum_all

    @pl.when(step == n_blocks // nbs - 1)
    def _():
        qbd = qbd_ref[...]
        scores = jnp.dot(qbd, ksum_all * (1.0 / MOBA_BLOCK), preferred_element_type=F32,
                         precision=HIGHEST)
        bias = _top_blocks(scores, n_blocks, 1)
        lo = _dot(qb, ktn_ref[...].astype(BF16))
        r_q = lax.broadcasted_iota(jnp.int32, (rows, t_new), 0) % t_new
        c_t = lax.broadcasted_iota(jnp.int32, (rows, t_new), 1)
        lo = jnp.where(c_t <= r_q, lo, NEG)
        m_own = jnp.max(lo, axis=1, keepdims=True)
        p_own = jnp.exp(lo - m_own)
        l_own = jnp.sum(p_own, axis=1, keepdims=True)
        o_own = _dot(p_own.astype(BF16), vn_ref[...].astype(BF16))
        m_sel = m_all + bias
        m_tot = jnp.maximum(jnp.max(m_sel, axis=1, keepdims=True), m_own)
        w = jnp.exp(m_sel - m_tot)
        w_own = jnp.exp(m_own - m_tot)
        l_tot = jnp.sum(w * l_all, axis=1, keepdims=True) + w_own * l_own
        o_tot = w_own * o_own
        for j in range(n_blocks):
            o_tot = o_tot + w[:, j:j + 1] * oblk_ref[j]
        o_tot = o_tot / l_tot
        col_head = lax.broadcasted_iota(jnp.int32, (t_new, ATT_WIDTH), 1) // HEAD_DIM
        out = jnp.zeros((t_new, ATT_WIDTH), F32)
        for h in range(ATT_HEADS):
            out = jnp.where(col_head == h, o_tot[h * t_new:(h + 1) * t_new, :], out)
        o_ref[...] = out


def moba_sample(qkv_s, ktn, cache_kt, cache_vt, page_table, layer, t_new, att, row0):
    n_seq, n_pages = page_table.shape
    assert MOBA_BLOCK == 2 * cache_kt.shape[-1]
    n_blocks = n_pages // 2
    nbs = 4 if n_blocks % 4 == 0 else 1
    rows = ATT_HEADS * t_new
    page_shape = (None, None) + cache_kt.shape[2:]

    def page_spec(j):
        return pl.BlockSpec(page_shape, lambda b, s, pt: (layer, pt[b, 2 * nbs * s + j], 0, 0, 0))

    pages = [page_spec(j) for j in range(2 * nbs)]
    grid_spec = pltpu.PrefetchScalarGridSpec(
        num_scalar_prefetch=1,
        grid=(n_seq, n_blocks // nbs),
        in_specs=[pl.BlockSpec((t_new, ATT_WIDTH), lambda b, s, pt: (b, 0)),
                  pl.BlockSpec((None, ATT_WIDTH, t_new), lambda b, s, pt: (b, 0, 0)),
                  pl.BlockSpec((t_new, ATT_WIDTH), lambda b, s, pt: (b, 2))] + pages + pages
                 + [pl.BlockSpec(memory_space=pl.ANY)],
        out_specs=pl.BlockSpec((t_new, ATT_WIDTH), lambda b, s, pt: (row0 // t_new + b, 0)),
        scratch_shapes=[pltpu.VMEM((rows, ATT_WIDTH), F32),
                        pltpu.VMEM((ATT_WIDTH, LANES), F32),
                        pltpu.VMEM((rows, LANES), F32), pltpu.VMEM((rows, LANES), F32),
                        pltpu.VMEM((n_blocks, rows, ATT_WIDTH), F32)],
    )
    return pl.pallas_call(
        functools.partial(_moba_sample_kernel, n_blocks=n_blocks, nbs=nbs),
        grid_spec=grid_spec,
        out_shape=jax.ShapeDtypeStruct(att.shape, F32),
        input_output_aliases={4 + 4 * nbs: 0},
        compiler_params=_params("arbitrary", "arbitrary"),
        name="moba_sample",
    )(page_table, qkv_s, ktn, qkv_s, *([cache_kt] * (2 * nbs)), *([cache_vt] * (2 * nbs)), att)


def _shift_rows(x, d, fill):
    row = lax.broadcasted_iota(jnp.int32, x.shape, 0)
    return jnp.where(row >= d, pltpu.roll(x, d, 0), fill)


def _lru_kernel(u_ref, gate_ref, cbuf_ref, h0_ref, cw_ref, cb_ref, wa_ref, ba_ref, wx_ref, bx_ref, lam_ref,
                *rest, tc, n_chunks):
    rec_ref, hl_ref, xbuf, hc = rest[-4:]
    c = pl.program_id(1)

    @pl.when(c == 0)
    def _():
        xbuf[0:SUBLANES, :] = cbuf_ref[...]
        hc[...] = h0_ref[...]

    xbuf[SUBLANES:SUBLANES + tc, :] = u_ref[...]
    w = cw_ref[...]
    uc = cb_ref[...]
    for t in range(CONV_W):
        off = SUBLANES - (CONV_W - 1) + t
        uc = uc + xbuf[off:off + tc, :] * w[t:t + 1, :]
    xbuf[0:SUBLANES, :] = xbuf[tc:tc + SUBLANES, :]

    ub = uc.astype(BF16)
    gate_r = _sigmoid(_dot(ub, wa_ref[...]) + ba_ref[...])
    gate_i = _sigmoid(_dot(ub, wx_ref[...]) + bx_ref[...])
    log_a = -LRU_C * gate_r * _softplus(-lam_ref[...])
    a = jnp.exp(log_a)
    bterm = jnp.sqrt(-jnp.tanh(log_a) * (a * a + 1.0)) * (gate_i * uc)
    row = lax.broadcasted_iota(jnp.int32, a.shape, 0)
    bterm = bterm + jnp.where(row == 0, a * hc[...], 0.0)
    d = 1
    while d < tc:
        b_sh = _shift_rows(bterm, d, 0.0)
        a_sh = _shift_rows(a, d, 1.0)
        bterm = a * b_sh + bterm
        a = a * a_sh
        d *= 2
    h = bterm
    hc[...] = h[tc - 1:tc, :]
    rec_ref[...] = h * _gelu_tanh(gate_ref[...])

    @pl.when(c == n_chunks - 1)
    def _():
        hl_ref[...] = h[tc - 1:tc, :]


def lru_mixer(proj, row0, n_batch, seq, tc, cbuf, h0, cw, cb, wa, ba, wx, bx, lam, rec=None):
    n_chunks = seq // tc
    blk0 = row0 // tc
    carried = [] if rec is None else [rec]
    vec = lambda: pl.BlockSpec((1, LRU_WIDTH), lambda b, c: (0, 0))
    return pl.pallas_call(
        functools.partial(_lru_kernel, tc=tc, n_chunks=n_chunks),
        grid=(n_batch, n_chunks),
        in_specs=[pl.BlockSpec((tc, LRU_WIDTH), lambda b, c: (blk0 + b * n_chunks + c, 0)),
                  pl.BlockSpec((tc, LRU_WIDTH), lambda b, c: (blk0 + b * n_chunks + c, 1)),
                  pl.BlockSpec((None, SUBLANES, LRU_WIDTH), lambda b, c: (b, 0, 0)),
                  pl.BlockSpec((None, 1, LRU_WIDTH), lambda b, c: (b, 0, 0)),
                  pl.BlockSpec((CONV_W, LRU_WIDTH), lambda b, c: (0, 0)),
                  vec(),
                  pl.BlockSpec((LRU_WIDTH, LRU_WIDTH), lambda b, c: (0, 0)),
                  vec(),
                  pl.BlockSpec((LRU_WIDTH, LRU_WIDTH), lambda b, c: (0, 0)),
                  vec(), vec()] + [pl.BlockSpec(memory_space=pl.ANY)] * len(carried),
        out_specs=[pl.BlockSpec((tc, LRU_WIDTH), lambda b, c: (blk0 + b * n_chunks + c, 0)),
                   pl.BlockSpec((None, 1, LRU_WIDTH), lambda b, c: (b, 0, 0))],
        out_shape=[jax.ShapeDtypeStruct((proj.shape[0], LRU_WIDTH), F32),
                   jax.ShapeDtypeStruct((n_batch, 1, LRU_WIDTH), F32)],
        input_output_aliases={11: 0} if carried else {},
        scratch_shapes=[pltpu.VMEM((SUBLANES + tc, LRU_WIDTH), F32), pltpu.VMEM((1, LRU_WIDTH), F32)],
        compiler_params=_params("arbitrary", "arbitrary"),
        name="lru_mixer",
    )(proj, proj, cbuf, h0, cw, cb, wa, ba, wx, bx, lam, *carried)


def _expand_heads(v, g):
    lane = lax.broadcasted_iota(jnp.int32, (1, SSD_GROUP_W), 1)
    out = v[:, SSD_HPG * g + SSD_HPG - 1:SSD_HPG * g + SSD_HPG]
    for k in range(SSD_HPG - 2, -1, -1):
        out = jnp.where(lane < (k + 1) * SSD_HEAD_DIM, v[:, SSD_HPG * g + k:SSD_HPG * g + k + 1], out)
    return out


def _ssd_kernel(x_ref, zx_ref, cbuf_ref, h0_ref, wdt_ref, wdtt_ref, dtb_ref, dtbt_ref, alog_ref, alogt_ref,
                cw_ref, cb_ref, dskip_ref, ng_ref, *rest, tc, n_chunks):
    y_ref, hl_ref, xbuf, st_ref = rest[-4:]
    c = pl.program_id(1)

    @pl.when(c == 0)
    def _():
        xbuf[0:SUBLANES, :] = cbuf_ref[...]
        for g in range(SSD_GROUPS):
            st_ref[g] = h0_ref[g * SSD_GROUP_W:(g + 1) * SSD_GROUP_W, :].T

    xbuf[SUBLANES:SUBLANES + tc, :] = zx_ref[:, SSD_INNER:]
    cw = cw_ref[...]
    cb = cb_ref[...]

    def conv_silu(lo, width):
        acc = cb[:, lo:lo + width]
        for t in range(CONV_W):
            off = SUBLANES - (CONV_W - 1) + t
            acc = acc + xbuf[off:off + tc, lo:lo + width] * cw[t:t + 1, lo:lo + width]
        return _silu(acc)

    xb = x_ref[...].astype(BF16)
    dt = _softplus(_dot(xb, wdt_ref[...]) + dtb_ref[...])
    dtt = _softplus(_dot_nt(wdtt_ref[...], xb) + dtbt_ref[...])
    a_neg = -jnp.exp(alog_ref[...])
    a_negt = -jnp.exp(alogt_ref[...])
    r = lax.broadcasted_iota(jnp.int32, (tc, tc), 0)
    s = lax.broadcasted_iota(jnp.int32, (tc, tc), 1)
    causal = s <= r
    tri = causal.astype(F32)
    cs = jnp.dot(tri, dt * a_neg, preferred_element_type=F32, precision=HIGHEST)
    cst = _dot_nt(dtt * a_negt, tri, precision=HIGHEST)
    cs_end = cs[tc - 1:tc, :]
    w_end = jnp.exp(cs_end - cs) * dt
    ecs = jnp.exp(cs)
    chunk_decay = jnp.exp(cs_end)
    lane = lax.broadcasted_iota(jnp.int32, (1, SSD_GROUP_W), 1)

    for g in range(SSD_GROUPS):
        xg = conv_silu(g * SSD_GROUP_W, SSD_GROUP_W)
        bg = conv_silu(SSD_INNER + g * SSD_STATE, SSD_STATE).astype(BF16)
        cg = conv_silu(SSD_INNER + SSD_GROUPS * SSD_STATE + g * SSD_STATE, SSD_STATE).astype(BF16)
        cbm = _dot_nt(cg, bg)
        y = jnp.zeros((tc, SSD_GROUP_W), F32)
        for k in range(SSD_HPG):
            e = SSD_HPG * g + k
            dec = jnp.exp(jnp.where(causal, cs[:, e:e + 1] - cst[e:e + 1, :], -jnp.inf))
            m = (cbm * dec * dtt[e:e + 1, :]).astype(BF16)
            head = jnp.logical_and(lane >= k * SSD_HEAD_DIM, lane < (k + 1) * SSD_HEAD_DIM)
            y = y + _dot(m, jnp.where(head, xg, 0.0).astype(BF16))
        st = st_ref[g]
        y = y + _dot(cg, st.astype(BF16)) * _expand_heads(ecs, g)
        xw = (xg * _expand_heads(w_end, g)).astype(BF16)
        st_ref[g] = st * _expand_heads(chunk_decay, g) + _dot_tn(bg, xw)
        y = y + dskip_ref[:, g * SSD_GROUP_W:(g + 1) * SSD_GROUP_W] * xg
        y = y * _silu(zx_ref[:, g * SSD_GROUP_W:(g + 1) * SSD_GROUP_W])
        y = y * lax.rsqrt(jnp.mean(y * y, axis=-1, keepdims=True) + RMS_EPS)
        y_ref[:, g * SSD_GROUP_W:(g + 1) * SSD_GROUP_W] = y * ng_ref[:, g * SSD_GROUP_W:(g + 1) * SSD_GROUP_W]

    xbuf[0:SUBLANES, :] = xbuf[tc:tc + SUBLANES, :]

    @pl.when(c == n_chunks - 1)
    def _():
        for g in range(SSD_GROUPS):
            hl_ref[g * SSD_GROUP_W:(g + 1) * SSD_GROUP_W, :] = st_ref[g].T


def ssd_mixer(x, zx, row0, n_batch, seq, tc, cbuf, h0, layer, wdt, wdtt, dtb, alog, cw, cb, dskip, ng, y=None):
    n_chunks = seq // tc
    blk0 = row0 // tc
    m_state = SSD_HEADS * SSD_HEAD_DIM
    carried = [] if y is None else [y]
    const2 = lambda shape: pl.BlockSpec(shape, lambda b, c: (0, 0))
    return pl.pallas_call(
        functools.partial(_ssd_kernel, tc=tc, n_chunks=n_chunks),
        grid=(n_batch, n_chunks),
        in_specs=[pl.BlockSpec((tc, D_MODEL), lambda b, c: (blk0 + b * n_chunks + c, 0)),
                  pl.BlockSpec((tc, SSD_INNER + SSD_CONV_DIM), lambda b, c: (blk0 + b * n_chunks + c, 0)),
                  pl.BlockSpec((None, SUBLANES, SSD_CONV_DIM), lambda b, c: (b, 0, 0)),
                  pl.BlockSpec((None, None, m_state, SSD_STATE), lambda b, c: (layer, b, 0, 0)),
                  const2((D_MODEL, LANES)), const2((LANES, D_MODEL)),
                  const2((1, LANES)), const2((LANES, 1)), const2((1, LANES)), const2((LANES, 1)),
                  const2((CONV_W, SSD_CONV_DIM)), const2((1, SSD_CONV_DIM)),
                  const2((1, SSD_INNER)), const2((1, SSD_INNER))] + [pl.BlockSpec(memory_space=pl.ANY)] * len(carried),
        out_specs=[pl.BlockSpec((tc, SSD_INNER), lambda b, c: (blk0 + b * n_chunks + c, 0)),
                   pl.BlockSpec((None, m_state, SSD_STATE), lambda b, c: (b, 0, 0))],
        out_shape=[jax.ShapeDtypeStruct((x.shape[0], SSD_INNER), F32),
                   jax.ShapeDtypeStruct((n_batch, m_state, SSD_STATE), F32)],
        input_output_aliases={14: 0} if carried else {},
        scratch_shapes=[pltpu.VMEM((SUBLANES + tc, SSD_CONV_DIM), F32),
                        pltpu.VMEM((SSD_GROUPS, SSD_STATE, SSD_GROUP_W), F32)],
        compiler_params=_params("arbitrary", "arbitrary"),
        name="ssd_mixer",
    )(x, zx, cbuf, h0, wdt, wdtt, dtb.reshape(1, LANES), dtb.reshape(LANES, 1),
      alog.reshape(1, LANES), alog.reshape(LANES, 1), cw, cb, dskip, ng, *carried)


def _block_diag(w):
    n, d, e = w.shape
    eye = jnp.eye(n, dtype=w.dtype)
    return (eye[:, None, :, None] * w[:, :, None, :]).reshape(n * d, n * e)


def _conv_tail(buf):
    return jnp.pad(buf, ((0, 0), (SUBLANES - (CONV_W - 1), 0), (0, 0)))


def _last_rows(a, row0, n_batch, seq, col0, width):
    rows = [lax.slice(a, (row0 + b * seq + seq - (CONV_W - 1), col0), (row0 + (b + 1) * seq, col0 + width))
            for b in range(n_batch)]
    return jnp.stack(rows)


def kernel(x_prompt, x_sample, cache_k, cache_v, page_table, state_lru_h, state_lru_conv, state_ssm, state_ssd_conv, w_in_even, lru_conv_w, lru_conv_b, lru_w_a, lru_b_a, lru_w_x, lru_b_x, lru_lambda, w_out_even, ssd_w_in, ssd_conv_w, ssd_conv_b, ssd_dt_bias, ssd_a_log, ssd_d, ssd_norm_g, ssd_w_out, ffn_w1, ffn_w3, ffn_w2, moe_router, moe_w1, moe_w3, moe_w2, ln_mix_g, ln_mix_b, ln_ffn_g, ln_ffn_b):
    bp, sp, d = x_prompt.shape
    bs, ts, _ = x_sample.shape
    mp = bp * sp
    ms = bs * ts
    m = mp + ms
    tm = 640 if m % 640 == 0 else ts * 8
    tm_ffn = 832 if m % 832 == 0 else tm
    n_even, n_odd = w_in_even.shape[0], ssd_w_in.shape[0]
    x = jnp.concatenate([x_prompt.reshape(mp, d), x_sample.reshape(ms, d)], axis=0)

    cache_kt = cache_k.transpose(0, 1, 3, 4, 2)
    cache_vt = cache_v.transpose(0, 1, 3, 4, 2)
    ffn_w = (ffn_w1.astype(BF16), ffn_w3.astype(BF16), ffn_w2.astype(BF16))
    moe_w = (moe_w1.astype(BF16), moe_w3.astype(BF16), moe_w2.astype(BF16))
    m_state = SSD_HEADS * SSD_HEAD_DIM
    ssm_in = state_ssm.reshape(n_odd, bs, m_state, SSD_STATE)

    ks_s, vs_s = [], []
    lruh_p, lruh_s, lrucv_p, lrucv_s = [], [], [], []
    ssm_p, ssm_s, ssdcv_p, ssdcv_s = [], [], [], []
    zeros_lru_cbuf = jnp.zeros((bp, SUBLANES, LRU_WIDTH), F32)
    zeros_lru_h = jnp.zeros((bp, 1, LRU_WIDTH), F32)
    zeros_ssd_cbuf = jnp.zeros((bp, SUBLANES, SSD_CONV_DIM), F32)
    zeros_ssm = jnp.zeros((1, bp, m_state, SSD_STATE), F32)
    no_gates = jnp.zeros((m, LANES), F32)
    kv_all = None

    for layer in range(DEPTH):
        i = layer // 2
        if layer % 2 == 0:
            n_qkv = 3 * ATT_WIDTH
            w_in = w_in_even[i].astype(BF16)
            w_qkv = w_in[:, :n_qkv]
            qt, kt_all, vt_all = qkv_transposed(w_qkv.T, x, bp, sp, 512 if sp % 512 == 0 else sp, i, n_even, kv_all)
            kv_all = (kt_all, vt_all)
            proj = matmul(x, w_in[:, n_qkv:], 2 * LRU_WIDTH, tm, 2 * LRU_WIDTH)
            qkv_s = matmul(x[mp:], w_qkv, n_qkv, ms, n_qkv)
            k_s = qkv_s[:, ATT_WIDTH:2 * ATT_WIDTH]
            v_s = qkv_s[:, 2 * ATT_WIDTH:]
            ktn = k_s.reshape(bs, ts, ATT_WIDTH).transpose(0, 2, 1)
            att = moba_prompt(qt, kt_all, vt_all, i, m)
            att = moba_sample(qkv_s, ktn, cache_kt, cache_vt, page_table, i, ts, att, mp)

            wa = _block_diag(lru_w_a[i]).astype(BF16)
            wx = _block_diag(lru_w_x[i]).astype(BF16)
            lru_args = (lru_conv_w[i], lru_conv_b[i].reshape(1, -1), wa, lru_b_a[i].reshape(1, -1),
                        wx, lru_b_x[i].reshape(1, -1), lru_lambda[i].reshape(1, -1))
            tc_p = 256 if sp % 256 == 0 else sp
            rec, hl_p = lru_mixer(proj, 0, bp, sp, tc_p, zeros_lru_cbuf, zeros_lru_h, *lru_args)
            rec, hl_s = lru_mixer(proj, mp, bs, ts, ts, _conv_tail(state_lru_conv[i]),
                                  state_lru_h[i].reshape(bs, 1, LRU_WIDTH), *lru_args, rec=rec)
            w_out = w_out_even[i].astype(BF16)
            x = matmul_residual_ln([att, rec], [w_out[:ATT_WIDTH], w_out[ATT_WIDTH:]], x,
                                   ln_mix_g[layer], ln_mix_b[layer], tm)
            x = swiglu_residual_ln(x, no_gates, *ffn_w, i, ln_ffn_g[layer], ln_ffn_b[layer], tm_ffn, 2, False)

            ks_s.append(k_s.reshape(bs, ts, ATT_HEADS, HEAD_DIM))
            vs_s.append(v_s.reshape(bs, ts, ATT_HEADS, HEAD_DIM))
            lruh_p.append(hl_p.reshape(bp, LRU_WIDTH))
            lruh_s.append(hl_s.reshape(bs, LRU_WIDTH))
            lrucv_p.append(_last_rows(proj, 0, bp, sp, 0, LRU_WIDTH))
            lrucv_s.append(_last_rows(proj, mp, bs, ts, 0, LRU_WIDTH))
        else:
            w_in = ssd_w_in[i]
            n_zx = SSD_INNER + SSD_CONV_DIM
            zx = matmul(x, w_in[:, :n_zx].astype(BF16), n_zx, tm, 2048)
            wdt = jnp.pad(w_in[:, n_zx:], ((0, 0), (0, LANES - SSD_HEADS))).astype(BF16)
            pad_h = lambda v: jnp.pad(v, (0, LANES - SSD_HEADS))
            dskip = jnp.repeat(ssd_d[i], SSD_HEAD_DIM).reshape(1, SSD_INNER)
            ssd_args = (wdt, wdt.T, pad_h(ssd_dt_bias[i]), pad_h(ssd_a_log[i]), ssd_conv_w[i],
                        ssd_conv_b[i].reshape(1, -1), dskip, ssd_norm_g[i].reshape(1, -1))
            tc_p = math.gcd(sp, SSD_CHUNK)
            y, st_p = ssd_mixer(x, zx, 0, bp, sp, tc_p, zeros_ssd_cbuf, zeros_ssm, 0, *ssd_args)
            y, st_s = ssd_mixer(x, zx, mp, bs, ts, ts, _conv_tail(state_ssd_conv[i]), ssm_in, i, *ssd_args, y=y)
            router_t = jnp.pad(moe_router[i].T, ((0, LANES - N_EXPERTS), (0, 0)))
            x, gates = matmul_residual_ln([y], [ssd_w_out[i].astype(BF16)], x, ln_mix_g[layer], ln_mix_b[layer], tm,
                                          router_t=router_t)
            x = swiglu_residual_ln(x, gates, *moe_w, i, ln_ffn_g[layer], ln_ffn_b[layer], tm_ffn, N_EXPERTS, True)

            ssm_p.append(st_p.reshape(bp, SSD_HEADS, SSD_HEAD_DIM, SSD_STATE))
            ssm_s.append(st_s.reshape(bs, SSD_HEADS, SSD_HEAD_DIM, SSD_STATE))
            ssdcv_p.append(_last_rows(zx, 0, bp, sp, SSD_INNER, SSD_CONV_DIM))
            ssdcv_s.append(_last_rows(zx, mp, bs, ts, SSD_INNER, SSD_CONV_DIM))

    head_major = lambda a: a.reshape(n_even, bp, ATT_HEADS, HEAD_DIM, sp).transpose(0, 1, 4, 2, 3)
    return (x[:mp].reshape(bp, sp, d), x[mp:].reshape(bs, ts, d),
            head_major(kv_all[0]), head_major(kv_all[1]), jnp.stack(ks_s), jnp.stack(vs_s),
            jnp.stack(lruh_p), jnp.stack(lruh_s), jnp.stack(lrucv_p), jnp.stack(lrucv_s),
            jnp.stack(ssm_p), jnp.stack(ssm_s), jnp.stack(ssdcv_p), jnp.stack(ssdcv_s))
```

```python
import functools
import math

import jax
import jax.numpy as jnp
from jax import lax
from jax.experimental import pallas as pl
from jax.experimental.pallas import tpu as pltpu

F32 = jnp.float32
BF16 = jnp.bfloat16

D_MODEL = 1024
DEPTH = 4
ATT_HEADS = 8
HEAD_DIM = 64
ATT_WIDTH = ATT_HEADS * HEAD_DIM
MOBA_BLOCK = 256
MOBA_TOPK = 3
LRU_WIDTH = 512
LRU_BLOCKS = 8
LRU_C = 8.0
CONV_W = 4
SSD_INNER = 2048
SSD_HEAD_DIM = 64
SSD_HEADS = 32
SSD_GROUPS = 8
SSD_HPG = 4
SSD_STATE = 128
SSD_CHUNK = 128
SSD_GROUP_W = SSD_HPG * SSD_HEAD_DIM
SSD_CONV_DIM = SSD_INNER + 2 * SSD_GROUPS * SSD_STATE
N_EXPERTS = 8
DN_ALPHA = (2 * DEPTH) ** 0.25
LN_EPS = 1e-5
RMS_EPS = 1e-6

LANES = 128
SUBLANES = 8
VMEM_LIMIT = 56 * 1024 * 1024
NEG = -1e30
LOG2E = 1.4426950408889634
ONES_ROWS = 16
HIGHEST = lax.Precision.HIGHEST


def _params(*sem):
    return pltpu.CompilerParams(dimension_semantics=sem, vmem_limit_bytes=VMEM_LIMIT)


def _dot(a, b):
    return jnp.dot(a, b, preferred_element_type=F32)


def _dot_nt(a, b, precision=None):
    return lax.dot_general(a, b, (((1,), (1,)), ((), ())), preferred_element_type=F32, precision=precision)


def _dot_tn(a, b):
    return lax.dot_general(a, b, (((0,), (0,)), ((), ())), preferred_element_type=F32)


def _sigmoid(x):
    return 0.5 * jnp.tanh(0.5 * x) + 0.5


def _silu(x):
    return x * _sigmoid(x)


def _softplus(x):
    return jnp.maximum(x, 0.0) + jnp.log1p(jnp.exp(-jnp.abs(x)))


def _gelu_tanh(x):
    return 0.5 * x * (1.0 + jnp.tanh(math.sqrt(2.0 / math.pi) * (x + 0.044715 * (x * x * x))))


def _layer_norm(y, g, b):
    mu = jnp.mean(y, axis=-1, keepdims=True)
    d = y - mu
    var = jnp.mean(d * d, axis=-1, keepdims=True)
    return d * lax.rsqrt(var + LN_EPS) * g + b


def _mm_kernel(x_ref, w_ref, o_ref):
    o_ref[...] = _dot(x_ref[...].astype(BF16), w_ref[...])


def matmul(x, w, n_cols, tm, tn):
    m, k = x.shape
    return pl.pallas_call(
        _mm_kernel,
        grid=(n_cols // tn, m // tm),
        in_specs=[pl.BlockSpec((tm, k), lambda j, i: (i, 0)),
                  pl.BlockSpec((k, tn), lambda j, i: (0, j))],
        out_specs=pl.BlockSpec((tm, tn), lambda j, i: (i, j)),
        out_shape=jax.ShapeDtypeStruct((m, n_cols), F32),
        compiler_params=_params("arbitrary", "arbitrary"),
        name="matmul",
    )(x, w)


def _qkv_t_kernel(w_ref, x_ref, *rest):
    q_ref, k_ref, v_ref = rest[-3:]
    res = _dot_nt(w_ref[...], x_ref[...].astype(BF16))
    q_ref[...] = res[:ATT_WIDTH]
    k_ref[...] = res[ATT_WIDTH:2 * ATT_WIDTH]
    v_ref[...] = res[2 * ATT_WIDTH:]


def qkv_transposed(w_t, x, n_batch, seq, ts, layer, n_layers, kv_prev):
    n, k = w_t.shape
    nblk = seq // ts
    kv_shape = jax.ShapeDtypeStruct((n_layers, n_batch, ATT_WIDTH, seq), F32)
    kv_spec = pl.BlockSpec((None, None, ATT_WIDTH, ts), lambda b, s: (layer, b, 0, s))
    carried = [] if kv_prev is None else list(kv_prev)
    return pl.pallas_call(
        _qkv_t_kernel,
        grid=(n_batch, nblk),
        in_specs=[pl.BlockSpec((n, k), lambda b, s: (0, 0)),
                  pl.BlockSpec((ts, k), lambda b, s: (b * nblk + s, 0))]
                 + [pl.BlockSpec(memory_space=pl.ANY)] * len(carried),
        out_specs=[pl.BlockSpec((None, ATT_WIDTH, ts), lambda b, s: (b, 0, s)), kv_spec, kv_spec],
        out_shape=[jax.ShapeDtypeStruct((n_batch, ATT_WIDTH, seq), F32), kv_shape, kv_shape],
        input_output_aliases={2 + j: 1 + j for j in range(len(carried))},
        compiler_params=_params("arbitrary", "arbitrary"),
        name="qkv_transposed",
    )(w_t, x, *carried)


def _top2_gates(logits):
    lane = lax.broadcasted_iota(jnp.int32, logits.shape, 1)
    logits = jnp.where(lane < N_EXPERTS, logits, -jnp.inf)
    m1 = jnp.max(logits, axis=1, keepdims=True)
    i1 = jnp.min(jnp.where(logits == m1, lane, LANES), axis=1, keepdims=True)
    rest = jnp.where(lane == i1, -jnp.inf, logits)
    m2 = jnp.max(rest, axis=1, keepdims=True)
    i2 = jnp.min(jnp.where(rest == m2, lane, LANES), axis=1, keepdims=True)
    e2 = jnp.exp(m2 - m1)
    g1 = 1.0 / (1.0 + e2)
    g2 = e2 * g1
    return jnp.where(lane == i1, g1, 0.0) + jnp.where(lane == i2, g2, 0.0)


def _mm_ln_kernel(*refs, n_pairs, routed):
    a_refs = refs[:n_pairs]
    w_refs = refs[n_pairs:2 * n_pairs]
    r_ref, g_ref, b_ref = refs[2 * n_pairs:2 * n_pairs + 3]
    mix = _dot(a_refs[0][...].astype(BF16), w_refs[0][...])
    for a_ref, w_ref in zip(a_refs[1:], w_refs[1:]):
        mix = mix + _dot(a_ref[...].astype(BF16), w_ref[...])
    y = _layer_norm(DN_ALPHA * r_ref[...] + mix, g_ref[...], b_ref[...])
    if routed:
        rhi_ref, rlo_ref, o_ref, gates_ref = refs[2 * n_pairs + 3:]
        y_hi = y.astype(BF16)
        y_lo = (y - y_hi.astype(F32)).astype(BF16)
        logits = _dot_nt(y_hi, rhi_ref[...]) + (_dot_nt(y_lo, rhi_ref[...]) + _dot_nt(y_hi, rlo_ref[...]))
        gates_ref[...] = _top2_gates(logits)
    else:
        o_ref, = refs[2 * n_pairs + 3:]
    o_ref[...] = y


def matmul_residual_ln(acts, weights, resid, g, b, tm, router_t=None):
    m, d = resid.shape
    n_pairs = len(acts)
    routed = router_t is not None
    in_specs = [pl.BlockSpec((tm, a.shape[1]), lambda i: (i, 0)) for a in acts]
    in_specs += [pl.BlockSpec(w.shape, lambda i: (0, 0)) for w in weights]
    in_specs += [pl.BlockSpec((tm, d), lambda i: (i, 0)),
                 pl.BlockSpec((1, d), lambda i: (0, 0)),
                 pl.BlockSpec((1, d), lambda i: (0, 0))]
    out_specs = [pl.BlockSpec((tm, d), lambda i: (i, 0))]
    out_shape = [jax.ShapeDtypeStruct((m, d), F32)]
    extra = []
    if routed:
        in_specs += [pl.BlockSpec((LANES, d), lambda i: (0, 0))] * 2
        out_specs.append(pl.BlockSpec((tm, LANES), lambda i: (i, 0)))
        out_shape.append(jax.ShapeDtypeStruct((m, LANES), F32))
        r_hi = router_t.astype(BF16)
        extra = [r_hi, (router_t - r_hi.astype(F32)).astype(BF16)]
    out = pl.pallas_call(
        functools.partial(_mm_ln_kernel, n_pairs=n_pairs, routed=routed),
        grid=(m // tm,),
        in_specs=in_specs,
        out_specs=out_specs,
        out_shape=out_shape,
        compiler_params=_params("arbitrary"),
        name="matmul_residual_ln",
    )(*acts, *weights, resid, g.reshape(1, d), b.reshape(1, d), *extra)
    return out if routed else out[0]


def _ffn_kernel(x_ref, gate_ref, w1_ref, w3_ref, w2_ref, g_ref, b_ref, o_ref, xb_ref, acc_ref, *, n_e, gated):
    e = pl.program_id(1)

    @pl.when(e == 0)
    def _():
        xb_ref[...] = x_ref[...].astype(BF16)
        acc_ref[...] = jnp.zeros_like(acc_ref)

    xb = xb_ref[...]
    h = _silu(_dot(xb, w1_ref[...])) * _dot(xb, w3_ref[...])
    if gated:
        lane = lax.broadcasted_iota(jnp.int32, (1, LANES), 1)
        ge = jnp.sum(jnp.where(lane == e, gate_ref[...], 0.0), axis=1, keepdims=True)
        h = h * ge
    acc_ref[...] += _dot(h.astype(BF16), w2_ref[...])

    @pl.when(e == n_e - 1)
    def _():
        o_ref[...] = _layer_norm(DN_ALPHA * x_ref[...] + acc_ref[...], g_ref[...], b_ref[...])


def swiglu_residual_ln(x, gates, w1, w3, w2, layer, g, b, tm, n_e, gated):
    m, d = x.shape
    if gated:
        f = w1.shape[3]
        w13_spec = pl.BlockSpec((None, None, d, f), lambda i, e: (layer, e, 0, 0))
        w2_spec = pl.BlockSpec((None, None, f, d), lambda i, e: (layer, e, 0, 0))
    else:
        f = w1.shape[2] // n_e
        w13_spec = pl.BlockSpec((None, d, f), lambda i, e: (layer, 0, e))
        w2_spec = pl.BlockSpec((None, f, d), lambda i, e: (layer, e, 0))
    return pl.pallas_call(
        functools.partial(_ffn_kernel, n_e=n_e, gated=gated),
        grid=(m // tm, n_e),
        in_specs=[pl.BlockSpec((tm, d), lambda i, e: (i, 0)),
                  pl.BlockSpec((tm, LANES), lambda i, e: (i, 0)),
                  w13_spec, w13_spec, w2_spec,
                  pl.BlockSpec((1, d), lambda i, e: (0, 0)),
                  pl.BlockSpec((1, d), lambda i, e: (0, 0))],
        out_specs=pl.BlockSpec((tm, d), lambda i, e: (i, 0)),
        out_shape=jax.ShapeDtypeStruct((m, d), F32),
        scratch_shapes=[pltpu.VMEM((tm, d), BF16), pltpu.VMEM((tm, d), F32)],
        compiler_params=_params("arbitrary", "arbitrary"),
        name="swiglu_residual_ln",
    )(x, gates, w1, w3, w2, g.reshape(1, d), b.reshape(1, d))


def _top_blocks(scores, n_valid, axis):
    blk = lax.broadcasted_iota(jnp.int32, scores.shape, axis)
    s = jnp.where(blk < n_valid, scores, -jnp.inf)
    bias = jnp.full(scores.shape, NEG, F32)
    for _ in range(MOBA_TOPK):
        mx = jnp.max(s, axis=axis, keepdims=True)
        hit = jnp.logical_and(s == mx, mx > -jnp.inf)
        idx = jnp.min(jnp.where(hit, blk, scores.shape[axis]), axis=axis, keepdims=True)
        pick = blk == idx
        bias = jnp.where(pick, 0.0, bias)
        s = jnp.where(pick, -jnp.inf, s)
    return bias


def _moba_prompt_kernel(qt_ref, kt_ref, vt_ref, o_ref, k_ref, vte_ref, means_ref, bias_ref, mblk_ref, s_ref, p_ref,
                        *, nb, group):
    qi = pl.program_id(2)
    blk = MOBA_BLOCK
    tq = qt_ref.shape[1]
    extents = list(range(group, nb, group)) + [nb]

    def for_extent(fn):
        lo = 0
        for nk in extents:
            pl.when(jnp.logical_and(qi >= lo, qi < nk))(functools.partial(fn, nk))
            lo = nk

    @pl.when(qi == 0)
    def _():
        for n in range(nb):
            kn = kt_ref[:, n * blk:(n + 1) * blk].T
            k_ref[n * blk:(n + 1) * blk, :] = kn.astype(BF16)
            means_ref[n:n + 1, :] = jnp.mean(kn, axis=0, keepdims=True)
        for h in range(2):
            vte_ref[h, 0:HEAD_DIM, :] = vt_ref[h * HEAD_DIM:(h + 1) * HEAD_DIM, :].astype(BF16)
            vte_ref[h, HEAD_DIM:, :] = jnp.ones((ONES_ROWS, vt_ref.shape[1]), BF16)
        mblk_ref[...] = jnp.zeros_like(mblk_ref)

    qt = qt_ref[...]
    sub = lax.broadcasted_iota(jnp.int32, (LANES, 1), 0)
    blk_row = lax.broadcasted_iota(jnp.int32, (nb, tq), 0)
    ws, biases = [], []
    for h in range(2):
        head = jnp.logical_and(sub >= h * HEAD_DIM, sub < (h + 1) * HEAD_DIM)
        qth = jnp.where(head, qt, 0.0)
        scores = jnp.dot(means_ref[...], qth, preferred_element_type=F32, precision=HIGHEST)
        biases.append(jnp.where(blk_row == qi, 0.0, _top_blocks(scores, qi, 0)))
        ws.append((qth * (HEAD_DIM ** -0.5 * LOG2E)).astype(BF16))
    w_all = jnp.concatenate(ws, axis=1)
    bias_ref[...] = jnp.concatenate(biases, axis=1)

    def logits(nk):
        s = _dot(k_ref[0:nk * blk, :], w_all)
        s_ref[0:nk * blk, :] = s
        mblk_ref[0:nk, :] = jnp.max(s.reshape(nk, blk, 2 * tq), axis=1)

    for_extent(logits)
    n_used = extents[-1]
    for nk in reversed(extents[:-1]):
        n_used = jnp.where(qi < nk, nk, n_used)

    own = pl.ds(pl.multiple_of(qi * blk, blk), blk)
    key_i = lax.broadcasted_iota(jnp.int32, (blk, 2 * tq), 0)
    q_i = lax.broadcasted_iota(jnp.int32, (blk, 2 * tq), 1) % tq
    s_own = jnp.where(key_i <= q_i, s_ref[own, :], NEG)
    s_ref[own, :] = s_own
    mblk_ref[pl.ds(qi, 1), :] = jnp.max(s_own, axis=0, keepdims=True)
    m = jnp.max(mblk_ref[...] + bias_ref[...], axis=0, keepdims=True)

    def pass_exp(jb, carry):
        rows = pl.ds(pl.multiple_of(jb * blk, blk), blk)
        p_ref[rows, :] = jnp.exp2(s_ref[rows, :] + (bias_ref[pl.ds(jb, 1), :] - m)).astype(BF16)
        return carry

    lax.fori_loop(0, n_used, pass_exp, 0)

    def weighted_values(nk):
        acc = [_dot(vte_ref[h, :, 0:nk * blk], p_ref[0:nk * blk, h * tq:(h + 1) * tq]) for h in range(2)]
        out_t = jnp.concatenate([a[:HEAD_DIM] / a[HEAD_DIM:HEAD_DIM + 1] for a in acc], axis=0)
        o_ref[...] = out_t.T

    for_extent(weighted_values)


def moba_prompt(qt, kt_all, vt_all, layer, m_rows):
    n_batch, _, seq = qt.shape
    nb = seq // MOBA_BLOCK
    tq = MOBA_BLOCK
    n_pairs = ATT_WIDTH // LANES
    kv_spec = pl.BlockSpec((None, None, LANES, seq), lambda b, p, i: (layer, b, p, 0))
    return pl.pallas_call(
        functools.partial(_moba_prompt_kernel, nb=nb, group=4),
        grid=(n_batch, n_pairs, nb),
        in_specs=[pl.BlockSpec((None, LANES, tq), lambda b, p, i: (b, p, i)), kv_spec, kv_spec],
        out_specs=pl.BlockSpec((tq, LANES), lambda b, p, i: (b * nb + i, p)),
        out_shape=jax.ShapeDtypeStruct((m_rows, ATT_WIDTH), F32),
        scratch_shapes=[pltpu.VMEM((seq, LANES), BF16),
                        pltpu.VMEM((2, HEAD_DIM + ONES_ROWS, seq), BF16),
                        pltpu.VMEM((nb, LANES), F32),
                        pltpu.VMEM((nb, 2 * tq), F32),
                        pltpu.VMEM((nb, 2 * tq), F32),
                        pltpu.VMEM((seq, 2 * tq), F32),
                        pltpu.VMEM((seq, 2 * tq), BF16)],
        compiler_params=_params("arbitrary", "arbitrary", "arbitrary"),
        name="moba_prompt",
    )(qt, kt_all, vt_all)


def _moba_sample_kernel(pt_ref, q_ref, ktn_ref, vn_ref, *rest, n_blocks, nbs):
    del pt_ref
    n_pages = 2 * nbs
    k_refs, v_refs = rest[:n_pages], rest[n_pages:2 * n_pages]
    o_ref, qbd_ref, ksum_ref, m_ref, l_ref, oblk_ref = rest[2 * n_pages + 1:]
    step = pl.program_id(1)
    t_new = q_ref.shape[0]
    rows = ATT_HEADS * t_new
    lane = lax.broadcasted_iota(jnp.int32, (1, LANES), 1)

    @pl.when(step == 0)
    def _():
        q = q_ref[...]
        col_head = lax.broadcasted_iota(jnp.int32, (t_new, ATT_WIDTH), 1) // HEAD_DIM
        for h in range(ATT_HEADS):
            qbd_ref[h * t_new:(h + 1) * t_new, :] = jnp.where(col_head == h, q, 0.0)
        ksum_ref[...] = jnp.zeros_like(ksum_ref)
        m_ref[...] = jnp.zeros_like(m_ref)
        l_ref[...] = jnp.zeros_like(l_ref)

    qb = (qbd_ref[...] * (HEAD_DIM ** -0.5)).astype(BF16)
    m_all, l_all, ksum_all = m_ref[...], l_ref[...], ksum_ref[...]
    kt_all = jnp.concatenate([k_ref[...].reshape(ATT_WIDTH, LANES) for k_ref in k_refs], axis=1)
    logits_all = _dot(qb, kt_all.astype(BF16))
    for j in range(nbs):
        n = step * nbs + j
        kt = kt_all[:, j * MOBA_BLOCK:(j + 1) * MOBA_BLOCK]
        vt = jnp.concatenate([v_refs[2 * j][...].reshape(ATT_WIDTH, LANES),
                              v_refs[2 * j + 1][...].reshape(ATT_WIDTH, LANES)], axis=1)
        logits = logits_all[:, j * MOBA_BLOCK:(j + 1) * MOBA_BLOCK]
        mx = jnp.max(logits, axis=1, keepdims=True)
        p = jnp.exp(logits - mx)
        oblk_ref[n] = _dot_nt(p.astype(BF16), vt.astype(BF16))
        m_all = jnp.where(lane == n, mx, m_all)
        l_all = jnp.where(lane == n, jnp.sum(p, axis=1, keepdims=True), l_all)
        ksum_all = jnp.where(lane == n, jnp.sum(kt, axis=1, keepdims=True), ksum_all)
    m_ref[...] = m_all
    l_ref[...] = l_all
    ksum_ref[...] = ksum_all

    @pl.when(step == n_blocks // nbs - 1)
    def _():
        qbd = qbd_ref[...]
        scores = jnp.dot(qbd, ksum_all * (1.0 / MOBA_BLOCK), preferred_element_type=F32,
                         precision=HIGHEST)
        bias = _top_blocks(scores, n_blocks, 1)
        lo = _dot(qb, ktn_ref[...].astype(BF16))
        r_q = lax.broadcasted_iota(jnp.int32, (rows, t_new), 0) % t_new
        c_t = lax.broadcasted_iota(jnp.int32, (rows, t_new), 1)
        lo = jnp.where(c_t <= r_q, lo, NEG)
        m_own = jnp.max(lo, axis=1, keepdims=True)
        p_own = jnp.exp(lo - m_own)
        l_own = jnp.sum(p_own, axis=1, keepdims=True)
        o_own = _dot(p_own.astype(BF16), vn_ref[...].astype(BF16))
        m_sel = m_all + bias
        m_tot = jnp.maximum(jnp.max(m_sel, axis=1, keepdims=True), m_own)
        w = jnp.exp(m_sel - m_tot)
        w_own = jnp.exp(m_own - m_tot)
        l_tot = jnp.sum(w * l_all, axis=1, keepdims=True) + w_own * l_own
        o_tot = w_own * o_own
        for j in range(n_blocks):
            o_tot = o_tot + w[:, j:j + 1] * oblk_ref[j]
        o_tot = o_tot / l_tot
        col_head = lax.broadcasted_iota(jnp.int32, (t_new, ATT_WIDTH), 1) // HEAD_DIM
        out = jnp.zeros((t_new, ATT_WIDTH), F32)
        for h in range(ATT_HEADS):
            out = jnp.where(col_head == h, o_tot[h * t_new:(h + 1) * t_new, :], out)
        o_ref[...] = out


def moba_sample(qkv_s, ktn, cache_kt, cache_vt, page_table, layer, t_new, att, row0):
    n_seq, n_pages = page_table.shape
    assert MOBA_BLOCK == 2 * cache_kt.shape[-1]
    n_blocks = n_pages // 2
    nbs = 4 if n_blocks % 4 == 0 else 1
    rows = ATT_HEADS * t_new
    page_shape = (None, None) + cache_kt.shape[2:]

    def page_spec(j):
        return pl.BlockSpec(page_shape, lambda b, s, pt: (layer, pt[b, 2 * nbs * s + j], 0, 0, 0))

    pages = [page_spec(j) for j in range(2 * nbs)]
    grid_spec = pltpu.PrefetchScalarGridSpec(
        num_scalar_prefetch=1,
        grid=(n_seq, n_blocks // nbs),
        in_specs=[pl.BlockSpec((t_new, ATT_WIDTH), lambda b, s, pt: (b, 0)),
                  pl.BlockSpec((None, ATT_WIDTH, t_new), lambda b, s, pt: (b, 0, 0)),
                  pl.BlockSpec((t_new, ATT_WIDTH), lambda b, s, pt: (b, 2))] + pages + pages
                 + [pl.BlockSpec(memory_space=pl.ANY)],
        out_specs=pl.BlockSpec((t_new, ATT_WIDTH), lambda b, s, pt: (row0 // t_new + b, 0)),
        scratch_shapes=[pltpu.VMEM((rows, ATT_WIDTH), F32),
                        pltpu.VMEM((ATT_WIDTH, LANES), F32),
                        pltpu.VMEM((rows, LANES), F32), pltpu.VMEM((rows, LANES), F32),
                        pltpu.VMEM((n_blocks, rows, ATT_WIDTH), F32)],
    )
    return pl.pallas_call(
        functools.partial(_moba_sample_kernel, n_blocks=n_blocks, nbs=nbs),
        grid_spec=grid_spec,
        out_shape=jax.ShapeDtypeStruct(att.shape, F32),
        input_output_aliases={4 + 4 * nbs: 0},
        compiler_params=_params("arbitrary", "arbitrary"),
        name="moba_sample",
    )(page_table, qkv_s, ktn, qkv_s, *([cache_kt] * (2 * nbs)), *([cache_vt] * (2 * nbs)), att)


def _shift_rows(x, d, fill):
    row = lax.broadcasted_iota(jnp.int32, x.shape, 0)
    return jnp.where(row >= d, pltpu.roll(x, d, 0), fill)


def _lru_kernel(u_ref, gate_ref, cbuf_ref, h0_ref, cw_ref, cb_ref, wa_ref, ba_ref, wx_ref, bx_ref, lam_ref,
                *rest, tc, n_chunks):
    rec_ref, hl_ref, xbuf, hc = rest[-4:]
    c = pl.program_id(1)

    @pl.when(c == 0)
    def _():
        xbuf[0:SUBLANES, :] = cbuf_ref[...]
        hc[...] = h0_ref[...]

    xbuf[SUBLANES:SUBLANES + tc, :] = u_ref[...]
    w = cw_ref[...]
    uc = cb_ref[...]
    for t in range(CONV_W):
        off = SUBLANES - (CONV_W - 1) + t
        uc = uc + xbuf[off:off + tc, :] * w[t:t + 1, :]
    xbuf[0:SUBLANES, :] = xbuf[tc:tc + SUBLANES, :]

    ub = uc.astype(BF16)
    gate_r = _sigmoid(_dot(ub, wa_ref[...]) + ba_ref[...])
    gate_i = _sigmoid(_dot(ub, wx_ref[...]) + bx_ref[...])
    log_a = -LRU_C * gate_r * _softplus(-lam_ref[...])
    a = jnp.exp(log_a)
    bterm = jnp.sqrt(-jnp.tanh(log_a) * (a * a + 1.0)) * (gate_i * uc)
    row = lax.broadcasted_iota(jnp.int32, a.shape, 0)
    bterm = bterm + jnp.where(row == 0, a * hc[...], 0.0)
    d = 1
    while d < tc:
        b_sh = _shift_rows(bterm, d, 0.0)
        a_sh = _shift_rows(a, d, 1.0)
        bterm = a * b_sh + bterm
        a = a * a_sh
        d *= 2
    h = bterm
    hc[...] = h[tc - 1:tc, :]
    rec_ref[...] = h * _gelu_tanh(gate_ref[...])

    @pl.when(c == n_chunks - 1)
    def _():
        hl_ref[...] = h[tc - 1:tc, :]


def lru_mixer(proj, row0, n_batch, seq, tc, cbuf, h0, cw, cb, wa, ba, wx, bx, lam, rec=None):
    n_chunks = seq // tc
    blk0 = row0 // tc
    carried = [] if rec is None else [rec]
    vec = lambda: pl.BlockSpec((1, LRU_WIDTH), lambda b, c: (0, 0))
    return pl.pallas_call(
        functools.partial(_lru_kernel, tc=tc, n_chunks=n_chunks),
        grid=(n_batch, n_chunks),
        in_specs=[pl.BlockSpec((tc, LRU_WIDTH), lambda b, c: (blk0 + b * n_chunks + c, 0)),
                  pl.BlockSpec((tc, LRU_WIDTH), lambda b, c: (blk0 + b * n_chunks + c, 1)),
                  pl.BlockSpec((None, SUBLANES, LRU_WIDTH), lambda b, c: (b, 0, 0)),
                  pl.BlockSpec((None, 1, LRU_WIDTH), lambda b, c: (b, 0, 0)),
                  pl.BlockSpec((CONV_W, LRU_WIDTH), lambda b, c: (0, 0)),
                  vec(),
                  pl.BlockSpec((LRU_WIDTH, LRU_WIDTH), lambda b, c: (0, 0)),
                  vec(),
                  pl.BlockSpec((LRU_WIDTH, LRU_WIDTH), lambda b, c: (0, 0)),
                  vec(), vec()] + [pl.BlockSpec(memory_space=pl.ANY)] * len(carried),
        out_specs=[pl.BlockSpec((tc, LRU_WIDTH), lambda b, c: (blk0 + b * n_chunks + c, 0)),
                   pl.BlockSpec((None, 1, LRU_WIDTH), lambda b, c: (b, 0, 0))],
        out_shape=[jax.ShapeDtypeStruct((proj.shape[0], LRU_WIDTH), F32),
                   jax.ShapeDtypeStruct((n_batch, 1, LRU_WIDTH), F32)],
        input_output_aliases={11: 0} if carried else {},
        scratch_shapes=[pltpu.VMEM((SUBLANES + tc, LRU_WIDTH), F32), pltpu.VMEM((1, LRU_WIDTH), F32)],
        compiler_params=_params("arbitrary", "arbitrary"),
        name="lru_mixer",
    )(proj, proj, cbuf, h0, cw, cb, wa, ba, wx, bx, lam, *carried)


def _expand_heads(v, g):
    lane = lax.broadcasted_iota(jnp.int32, (1, SSD_GROUP_W), 1)
    out = v[:, SSD_HPG * g + SSD_HPG - 1:SSD_HPG * g + SSD_HPG]
    for k in range(SSD_HPG - 2, -1, -1):
        out = jnp.where(lane < (k + 1) * SSD_HEAD_DIM, v[:, SSD_HPG * g + k:SSD_HPG * g + k + 1], out)
    return out


def _ssd_kernel(x_ref, zx_ref, cbuf_ref, h0_ref, wdt_ref, wdtt_ref, dtb_ref, dtbt_ref, alog_ref, alogt_ref,
                cw_ref, cb_ref, dskip_ref, ng_ref, *rest, tc, n_chunks):
    y_ref, hl_ref, xbuf, st_ref = rest[-4:]
    c = pl.program_id(1)

    @pl.when(c == 0)
    def _():
        xbuf[0:SUBLANES, :] = cbuf_ref[...]
        for g in range(SSD_GROUPS):
            st_ref[g] = h0_ref[g * SSD_GROUP_W:(g + 1) * SSD_GROUP_W, :].T

    xbuf[SUBLANES:SUBLANES + tc, :] = zx_ref[:, SSD_INNER:]
    cw = cw_ref[...]
    cb = cb_ref[...]

    def conv_silu(lo, width):
        acc = cb[:, lo:lo + width]
        for t in range(CONV_W):
            off = SUBLANES - (CONV_W - 1) + t
            acc = acc + xbuf[off:off + tc, lo:lo + width] * cw[t:t + 1, lo:lo + width]
        return _silu(acc)

    xb = x_ref[...].astype(BF16)
    dt = _softplus(_dot(xb, wdt_ref[...]) + dtb_ref[...])
    dtt = _softplus(_dot_nt(wdtt_ref[...], xb) + dtbt_ref[...])
    a_neg = -jnp.exp(alog_ref[...])
    a_negt = -jnp.exp(alogt_ref[...])
    r = lax.broadcasted_iota(jnp.int32, (tc, tc), 0)
    s = lax.broadcasted_iota(jnp.int32, (tc, tc), 1)
    causal = s <= r
    tri = causal.astype(F32)
    cs = jnp.dot(tri, dt * a_neg, preferred_element_type=F32, precision=HIGHEST)
    cst = _dot_nt(dtt * a_negt, tri, precision=HIGHEST)
    cs_end = cs[tc - 1:tc, :]
    w_end = jnp.exp(cs_end - cs) * dt
    ecs = jnp.exp(cs)
    chunk_decay = jnp.exp(cs_end)
    lane = lax.broadcasted_iota(jnp.int32, (1, SSD_GROUP_W), 1)

    for g in range(SSD_GROUPS):
        xg = conv_silu(g * SSD_GROUP_W, SSD_GROUP_W)
        bg = conv_silu(SSD_INNER + g * SSD_STATE, SSD_STATE).astype(BF16)
        cg = conv_silu(SSD_INNER + SSD_GROUPS * SSD_STATE + g * SSD_STATE, SSD_STATE).astype(BF16)
        cbm = _dot_nt(cg, bg)
        y = jnp.zeros((tc, SSD_GROUP_W), F32)
        for k in range(SSD_HPG):
            e = SSD_HPG * g + k
            dec = jnp.exp(jnp.where(causal, cs[:, e:e + 1] - cst[e:e + 1, :], -jnp.inf))
            m = (cbm * dec * dtt[e:e + 1, :]).astype(BF16)
            head = jnp.logical_and(lane >= k * SSD_HEAD_DIM, lane < (k + 1) * SSD_HEAD_DIM)
            y = y + _dot(m, jnp.where(head, xg, 0.0).astype(BF16))
        st = st_ref[g]
        y = y + _dot(cg, st.astype(BF16)) * _expand_heads(ecs, g)
        xw = (xg * _expand_heads(w_end, g)).astype(BF16)
        st_ref[g] = st * _expand_heads(chunk_decay, g) + _dot_tn(bg, xw)
        y = y + dskip_ref[:, g * SSD_GROUP_W:(g + 1) * SSD_GROUP_W] * xg
        y = y * _silu(zx_ref[:, g * SSD_GROUP_W:(g + 1) * SSD_GROUP_W])
        y = y * lax.rsqrt(jnp.mean(y * y, axis=-1, keepdims=True) + RMS_EPS)
        y_ref[:, g * SSD_GROUP_W:(g + 1) * SSD_GROUP_W] = y * ng_ref[:, g * SSD_GROUP_W:(g + 1) * SSD_GROUP_W]

    xbuf[0:SUBLANES, :] = xbuf[tc:tc + SUBLANES, :]

    @pl.when(c == n_chunks - 1)
    def _():
        for g in range(SSD_GROUPS):
            hl_ref[g * SSD_GROUP_W:(g + 1) * SSD_GROUP_W, :] = st_ref[g].T


def ssd_mixer(x, zx, row0, n_batch, seq, tc, cbuf, h0, layer, wdt, wdtt, dtb, alog, cw, cb, dskip, ng, y=None):
    n_chunks = seq // tc
    blk0 = row0 // tc
    m_state = SSD_HEADS * SSD_HEAD_DIM
    carried = [] if y is None else [y]
    const2 = lambda shape: pl.BlockSpec(shape, lambda b, c: (0, 0))
    return pl.pallas_call(
        functools.partial(_ssd_kernel, tc=tc, n_chunks=n_chunks),
        grid=(n_batch, n_chunks),
        in_specs=[pl.BlockSpec((tc, D_MODEL), lambda b, c: (blk0 + b * n_chunks + c, 0)),
                  pl.BlockSpec((tc, SSD_INNER + SSD_CONV_DIM), lambda b, c: (blk0 + b * n_chunks + c, 0)),
                  pl.BlockSpec((None, SUBLANES, SSD_CONV_DIM), lambda b, c: (b, 0, 0)),
                  pl.BlockSpec((None, None, m_state, SSD_STATE), lambda b, c: (layer, b, 0, 0)),
                  const2((D_MODEL, LANES)), const2((LANES, D_MODEL)),
                  const2((1, LANES)), const2((LANES, 1)), const2((1, LANES)), const2((LANES, 1)),
                  const2((CONV_W, SSD_CONV_DIM)), const2((1, SSD_CONV_DIM)),
                  const2((1, SSD_INNER)), const2((1, SSD_INNER))] + [pl.BlockSpec(memory_space=pl.ANY)] * len(carried),
        out_specs=[pl.BlockSpec((tc, SSD_INNER), lambda b, c: (blk0 + b * n_chunks + c, 0)),
                   pl.BlockSpec((None, m_state, SSD_STATE), lambda b, c: (b, 0, 0))],
        out_shape=[jax.ShapeDtypeStruct((x.shape[0], SSD_INNER), F32),
                   jax.ShapeDtypeStruct((n_batch, m_state, SSD_STATE), F32)],
        input_output_aliases={14: 0} if carried else {},
        scratch_shapes=[pltpu.VMEM((SUBLANES + tc, SSD_CONV_DIM), F32),
                        pltpu.VMEM((SSD_GROUPS, SSD_STATE, SSD_GROUP_W), F32)],
        compiler_params=_params("arbitrary", "arbitrary"),
        name="ssd_mixer",
    )(x, zx, cbuf, h0, wdt, wdtt, dtb.reshape(1, LANES), dtb.reshape(LANES, 1),
      alog.reshape(1, LANES), alog.reshape(LANES, 1), cw, cb, dskip, ng, *carried)


def _block_diag(w):
    n, d, e = w.shape
    eye = jnp.eye(n, dtype=w.dtype)
    return (eye[:, None, :, None] * w[:, :, None, :]).reshape(n * d, n * e)


def _conv_tail(buf):
    return jnp.pad(buf, ((0, 0), (SUBLANES - (CONV_W - 1), 0), (0, 0)))


def _last_rows(a, row0, n_batch, seq, col0, width):
    tail = CONV_W - 1
    if seq <= 4 * tail:
        block = lax.slice(a, (row0, col0), (row0 + n_batch * seq, col0 + width))
        return block.reshape(n_batch, seq, width)[:, seq - tail:]
    rows = [lax.slice(a, (row0 + b * seq + seq - tail, col0), (row0 + (b + 1) * seq, col0 + width))
            for b in range(n_batch)]
    return jnp.stack(rows)


def kernel(x_prompt, x_sample, cache_k, cache_v, page_table, state_lru_h, state_lru_conv, state_ssm, state_ssd_conv, w_in_even, lru_conv_w, lru_conv_b, lru_w_a, lru_b_a, lru_w_x, lru_b_x, lru_lambda, w_out_even, ssd_w_in, ssd_conv_w, ssd_conv_b, ssd_dt_bias, ssd_a_log, ssd_d, ssd_norm_g, ssd_w_out, ffn_w1, ffn_w3, ffn_w2, moe_router, moe_w1, moe_w3, moe_w2, ln_mix_g, ln_mix_b, ln_ffn_g, ln_ffn_b):
    bp, sp, d = x_prompt.shape
    bs, ts, _ = x_sample.shape
    mp = bp * sp
    ms = bs * ts
    m = mp + ms
    tm = 640 if m % 640 == 0 else ts * 8
    tm_ffn = 832 if m % 832 == 0 else tm
    n_even, n_odd = w_in_even.shape[0], ssd_w_in.shape[0]
    x = jnp.concatenate([x_prompt.reshape(mp, d), x_sample.reshape(ms, d)], axis=0)

    cache_kt = cache_k.transpose(0, 1, 3, 4, 2)
    cache_vt = cache_v.transpose(0, 1, 3, 4, 2)
    ffn_w = (ffn_w1.astype(BF16), ffn_w3.astype(BF16), ffn_w2.astype(BF16))
    moe_w = (moe_w1.astype(BF16), moe_w3.astype(BF16), moe_w2.astype(BF16))
    m_state = SSD_HEADS * SSD_HEAD_DIM
    ssm_in = state_ssm.reshape(n_odd, bs, m_state, SSD_STATE)

    ks_s, vs_s = [], []
    lruh_p, lruh_s, lrucv_p, lrucv_s = [], [], [], []
    ssm_p, ssm_s, ssdcv_p, ssdcv_s = [], [], [], []
    zeros_lru_cbuf = jnp.zeros((bp, SUBLANES, LRU_WIDTH), F32)
    zeros_lru_h = jnp.zeros((bp, 1, LRU_WIDTH), F32)
    zeros_ssd_cbuf = jnp.zeros((bp, SUBLANES, SSD_CONV_DIM), F32)
    zeros_ssm = jnp.zeros((1, bp, m_state, SSD_STATE), F32)
    no_gates = jnp.zeros((m, LANES), F32)
    kv_all = None

    for layer in range(DEPTH):
        i = layer // 2
        if layer % 2 == 0:
            n_qkv = 3 * ATT_WIDTH
            w_in = w_in_even[i].astype(BF16)
            w_qkv = w_in[:, :n_qkv]
            qt, kt_all, vt_all = qkv_transposed(w_qkv.T, x, bp, sp, 512 if sp % 512 == 0 else sp, i, n_even, kv_all)
            kv_all = (kt_all, vt_all)
            proj = matmul(x, w_in[:, n_qkv:], 2 * LRU_WIDTH, tm, 2 * LRU_WIDTH)
            qkv_s = matmul(x[mp:], w_qkv, n_qkv, ms, n_qkv)
            k_s = qkv_s[:, ATT_WIDTH:2 * ATT_WIDTH]
            v_s = qkv_s[:, 2 * ATT_WIDTH:]
            ktn = k_s.reshape(bs, ts, ATT_WIDTH).transpose(0, 2, 1)
            att = moba_prompt(qt, kt_all, vt_all, i, m)
            att = moba_sample(qkv_s, ktn, cache_kt, cache_vt, page_table, i, ts, att, mp)

            wa = _block_diag(lru_w_a[i]).astype(BF16)
            wx = _block_diag(lru_w_x[i]).astype(BF16)
            lru_args = (lru_conv_w[i], lru_conv_b[i].reshape(1, -1), wa, lru_b_a[i].reshape(1, -1),
                        wx, lru_b_x[i].reshape(1, -1), lru_lambda[i].reshape(1, -1))
            tc_p = 256 if sp % 256 == 0 else sp
            rec, hl_p = lru_mixer(proj, 0, bp, sp, tc_p, zeros_lru_cbuf, zeros_lru_h, *lru_args)
            rec, hl_s = lru_mixer(proj, mp, bs, ts, ts, _conv_tail(state_lru_conv[i]),
                                  state_lru_h[i].reshape(bs, 1, LRU_WIDTH), *lru_args, rec=rec)
            w_out = w_out_even[i].astype(BF16)
            x = matmul_residual_ln([att, rec], [w_out[:ATT_WIDTH], w_out[ATT_WIDTH:]], x,
                                   ln_mix_g[layer], ln_mix_b[layer], tm)
            x = swiglu_residual_ln(x, no_gates, *ffn_w, i, ln_ffn_g[layer], ln_ffn_b[layer], tm_ffn, 2, False)

            ks_s.append(k_s.reshape(bs, ts, ATT_HEADS, HEAD_DIM))
            vs_s.append(v_s.reshape(bs, ts, ATT_HEADS, HEAD_DIM))
            lruh_p.append(hl_p.reshape(bp, LRU_WIDTH))
            lruh_s.append(hl_s.reshape(bs, LRU_WIDTH))
            lrucv_p.append(_last_rows(proj, 0, bp, sp, 0, LRU_WIDTH))
            lrucv_s.append(_last_rows(proj, mp, bs, ts, 0, LRU_WIDTH))
        else:
            w_in = ssd_w_in[i]
            n_zx = SSD_INNER + SSD_CONV_DIM
            zx = matmul(x, w_in[:, :n_zx].astype(BF16), n_zx, tm, 2048)
            wdt = jnp.pad(w_in[:, n_zx:], ((0, 0), (0, LANES - SSD_HEADS))).astype(BF16)
            pad_h = lambda v: jnp.pad(v, (0, LANES - SSD_HEADS))
            dskip = jnp.repeat(ssd_d[i], SSD_HEAD_DIM).reshape(1, SSD_INNER)
            ssd_args = (wdt, wdt.T, pad_h(ssd_dt_bias[i]), pad_h(ssd_a_log[i]), ssd_conv_w[i],
                        ssd_conv_b[i].reshape(1, -1), dskip, ssd_norm_g[i].reshape(1, -1))
            tc_p = math.gcd(sp, SSD_CHUNK)
            y, st_p = ssd_mixer(x, zx, 0, bp, sp, tc_p, zeros_ssd_cbuf, zeros_ssm, 0, *ssd_args)
            y, st_s = ssd_mixer(x, zx, mp, bs, ts, ts, _conv_tail(state_ssd_conv[i]), ssm_in, i, *ssd_args, y=y)
            router_t = jnp.pad(moe_router[i].T, ((0, LANES - N_EXPERTS), (0, 0)))
            x, gates = matmul_residual_ln([y], [ssd_w_out[i].astype(BF16)], x, ln_mix_g[layer], ln_mix_b[layer], tm,
                                          router_t=router_t)
            x = swiglu_residual_ln(x, gates, *moe_w, i, ln_ffn_g[layer], ln_ffn_b[layer], tm_ffn, N_EXPERTS, True)

            ssm_p.append(st_p.reshape(bp, SSD_HEADS, SSD_HEAD_DIM, SSD_STATE))
            ssm_s.append(st_s.reshape(bs, SSD_HEADS, SSD_HEAD_DIM, SSD_STATE))
            ssdcv_p.append(_last_rows(zx, 0, bp, sp, SSD_INNER, SSD_CONV_DIM))
            ssdcv_s.append(_last_rows(zx, mp, bs, ts, SSD_INNER, SSD_CONV_DIM))

    head_major = lambda a: a.reshape(n_even, bp, ATT_HEADS, HEAD_DIM, sp).transpose(0, 1, 4, 2, 3)
    return (x[:mp].reshape(bp, sp, d), x[mp:].reshape(bs, ts, d),
            head_major(kv_all[0]), head_major(kv_all[1]), jnp.stack(ks_s), jnp.stack(vs_s),
            jnp.stack(lruh_p), jnp.stack(lruh_s), jnp.stack(lrucv_p), jnp.stack(lrucv_s),
            jnp.stack(ssm_p), jnp.stack(ssm_s), jnp.stack(ssdcv_p), jnp.stack(ssdcv_s))
```

```python
import functools
import math

import jax
import jax.numpy as jnp
from jax import lax
from jax.experimental import pallas as pl
from jax.experimental.pallas import tpu as pltpu

F32 = jnp.float32
BF16 = jnp.bfloat16

D_MODEL = 1024
DEPTH = 4
ATT_HEADS = 8
HEAD_DIM = 64
ATT_WIDTH = ATT_HEADS * HEAD_DIM
MOBA_BLOCK = 256
MOBA_TOPK = 3
LRU_WIDTH = 512
LRU_BLOCKS = 8
LRU_C = 8.0
CONV_W = 4
SSD_INNER = 2048
SSD_HEAD_DIM = 64
SSD_HEADS = 32
SSD_GROUPS = 8
SSD_HPG = 4
SSD_STATE = 128
SSD_CHUNK = 128
SSD_GROUP_W = SSD_HPG * SSD_HEAD_DIM
SSD_CONV_DIM = SSD_INNER + 2 * SSD_GROUPS * SSD_STATE
N_EXPERTS = 8
DN_ALPHA = (2 * DEPTH) ** 0.25
LN_EPS = 1e-5
RMS_EPS = 1e-6

LANES = 128
SUBLANES = 8
VMEM_LIMIT = 56 * 1024 * 1024
NEG = -1e30
LOG2E = 1.4426950408889634
ONES_ROWS = 16
HIGHEST = lax.Precision.HIGHEST


def _params(*sem):
    return pltpu.CompilerParams(dimension_semantics=sem, vmem_limit_bytes=VMEM_LIMIT)


def _dot(a, b):
    return jnp.dot(a, b, preferred_element_type=F32)


def _dot_nt(a, b, precision=None):
    return lax.dot_general(a, b, (((1,), (1,)), ((), ())), preferred_element_type=F32, precision=precision)


def _dot_tn(a, b):
    return lax.dot_general(a, b, (((0,), (0,)), ((), ())), preferred_element_type=F32)


def _sigmoid(x):
    return 0.5 * jnp.tanh(0.5 * x) + 0.5


def _silu(x):
    return x * _sigmoid(x)


def _softplus(x):
    return jnp.maximum(x, 0.0) + jnp.log1p(jnp.exp(-jnp.abs(x)))


def _gelu_tanh(x):
    return 0.5 * x * (1.0 + jnp.tanh(math.sqrt(2.0 / math.pi) * (x + 0.044715 * (x * x * x))))


def _layer_norm(y, g, b):
    mu = jnp.mean(y, axis=-1, keepdims=True)
    d = y - mu
    var = jnp.mean(d * d, axis=-1, keepdims=True)
    return d * lax.rsqrt(var + LN_EPS) * g + b


def _mm_kernel(x_ref, w_ref, o_ref):
    o_ref[...] = _dot(x_ref[...].astype(BF16), w_ref[...])


def matmul(x, w, n_cols, tm, tn):
    m, k = x.shape
    return pl.pallas_call(
        _mm_kernel,
        grid=(n_cols // tn, m // tm),
        in_specs=[pl.BlockSpec((tm, k), lambda j, i: (i, 0)),
                  pl.BlockSpec((k, tn), lambda j, i: (0, j))],
        out_specs=pl.BlockSpec((tm, tn), lambda j, i: (i, j)),
        out_shape=jax.ShapeDtypeStruct((m, n_cols), F32),
        compiler_params=_params("arbitrary", "arbitrary"),
        name="matmul",
    )(x, w)


def _qkv_t_kernel(w_ref, x_ref, *rest):
    q_ref, k_ref, v_ref = rest[-3:]
    res = _dot_nt(w_ref[...], x_ref[...].astype(BF16))
    q_ref[...] = res[:ATT_WIDTH]
    k_ref[...] = res[ATT_WIDTH:2 * ATT_WIDTH]
    v_ref[...] = res[2 * ATT_WIDTH:]


def qkv_transposed(w_t, x, n_batch, seq, ts, layer, n_layers, kv_prev):
    n, k = w_t.shape
    nblk = seq // ts
    kv_shape = jax.ShapeDtypeStruct((n_layers, n_batch, ATT_WIDTH, seq), F32)
    kv_spec = pl.BlockSpec((None, None, ATT_WIDTH, ts), lambda b, s: (layer, b, 0, s))
    carried = [] if kv_prev is None else list(kv_prev)
    return pl.pallas_call(
        _qkv_t_kernel,
        grid=(n_batch, nblk),
        in_specs=[pl.BlockSpec((n, k), lambda b, s: (0, 0)),
                  pl.BlockSpec((ts, k), lambda b, s: (b * nblk + s, 0))]
                 + [pl.BlockSpec(memory_space=pl.ANY)] * len(carried),
        out_specs=[pl.BlockSpec((None, ATT_WIDTH, ts), lambda b, s: (b, 0, s)), kv_spec, kv_spec],
        out_shape=[jax.ShapeDtypeStruct((n_batch, ATT_WIDTH, seq), F32), kv_shape, kv_shape],
        input_output_aliases={2 + j: 1 + j for j in range(len(carried))},
        compiler_params=_params("arbitrary", "arbitrary"),
        name="qkv_transposed",
    )(w_t, x, *carried)


def _top2_gates(logits):
    lane = lax.broadcasted_iota(jnp.int32, logits.shape, 1)
    logits = jnp.where(lane < N_EXPERTS, logits, -jnp.inf)
    m1 = jnp.max(logits, axis=1, keepdims=True)
    i1 = jnp.min(jnp.where(logits == m1, lane, LANES), axis=1, keepdims=True)
    rest = jnp.where(lane == i1, -jnp.inf, logits)
    m2 = jnp.max(rest, axis=1, keepdims=True)
    i2 = jnp.min(jnp.where(rest == m2, lane, LANES), axis=1, keepdims=True)
    e2 = jnp.exp(m2 - m1)
    g1 = 1.0 / (1.0 + e2)
    g2 = e2 * g1
    return jnp.where(lane == i1, g1, 0.0) + jnp.where(lane == i2, g2, 0.0)


def _mm_ln_kernel(*refs, n_pairs, routed):
    a_refs = refs[:n_pairs]
    w_refs = refs[n_pairs:2 * n_pairs]
    r_ref, g_ref, b_ref = refs[2 * n_pairs:2 * n_pairs + 3]
    mix = _dot(a_refs[0][...].astype(BF16), w_refs[0][...])
    for a_ref, w_ref in zip(a_refs[1:], w_refs[1:]):
        mix = mix + _dot(a_ref[...].astype(BF16), w_ref[...])
    y = _layer_norm(DN_ALPHA * r_ref[...] + mix, g_ref[...], b_ref[...])
    if routed:
        rhi_ref, rlo_ref, o_ref, gates_ref = refs[2 * n_pairs + 3:]
        y_hi = y.astype(BF16)
        y_lo = (y - y_hi.astype(F32)).astype(BF16)
        logits = _dot_nt(y_hi, rhi_ref[...]) + (_dot_nt(y_lo, rhi_ref[...]) + _dot_nt(y_hi, rlo_ref[...]))
        gates_ref[...] = _top2_gates(logits)
    else:
        o_ref, = refs[2 * n_pairs + 3:]
    o_ref[...] = y


def matmul_residual_ln(acts, weights, resid, g, b, tm, router_t=None):
    m, d = resid.shape
    n_pairs = len(acts)
    routed = router_t is not None
    in_specs = [pl.BlockSpec((tm, a.shape[1]), lambda i: (i, 0)) for a in acts]
    in_specs += [pl.BlockSpec(w.shape, lambda i: (0, 0)) for w in weights]
    in_specs += [pl.BlockSpec((tm, d), lambda i: (i, 0)),
                 pl.BlockSpec((1, d), lambda i: (0, 0)),
                 pl.BlockSpec((1, d), lambda i: (0, 0))]
    out_specs = [pl.BlockSpec((tm, d), lambda i: (i, 0))]
    out_shape = [jax.ShapeDtypeStruct((m, d), F32)]
    extra = []
    if routed:
        in_specs += [pl.BlockSpec((LANES, d), lambda i: (0, 0))] * 2
        out_specs.append(pl.BlockSpec((tm, LANES), lambda i: (i, 0)))
        out_shape.append(jax.ShapeDtypeStruct((m, LANES), F32))
        r_hi = router_t.astype(BF16)
        extra = [r_hi, (router_t - r_hi.astype(F32)).astype(BF16)]
    out = pl.pallas_call(
        functools.partial(_mm_ln_kernel, n_pairs=n_pairs, routed=routed),
        grid=(m // tm,),
        in_specs=in_specs,
        out_specs=out_specs,
        out_shape=out_shape,
        compiler_params=_params("arbitrary"),
        name="matmul_residual_ln",
    )(*acts, *weights, resid, g.reshape(1, d), b.reshape(1, d), *extra)
    return out if routed else out[0]


def _ffn_kernel(x_ref, gate_ref, w1_ref, w3_ref, w2_ref, g_ref, b_ref, o_ref, xb_ref, acc_ref, *, n_e, gated):
    e = pl.program_id(1)

    @pl.when(e == 0)
    def _():
        xb_ref[...] = x_ref[...].astype(BF16)
        acc_ref[...] = jnp.zeros_like(acc_ref)

    xb = xb_ref[...]
    h = _silu(_dot(xb, w1_ref[...])) * _dot(xb, w3_ref[...])
    if gated:
        lane = lax.broadcasted_iota(jnp.int32, (1, LANES), 1)
        ge = jnp.sum(jnp.where(lane == e, gate_ref[...], 0.0), axis=1, keepdims=True)
        h = h * ge
    acc_ref[...] += _dot(h.astype(BF16), w2_ref[...])

    @pl.when(e == n_e - 1)
    def _():
        o_ref[...] = _layer_norm(DN_ALPHA * x_ref[...] + acc_ref[...], g_ref[...], b_ref[...])


def swiglu_residual_ln(x, gates, w1, w3, w2, layer, g, b, tm, n_e, gated):
    m, d = x.shape
    if gated:
        f = w1.shape[3]
        w13_spec = pl.BlockSpec((None, None, d, f), lambda i, e: (layer, e, 0, 0))
        w2_spec = pl.BlockSpec((None, None, f, d), lambda i, e: (layer, e, 0, 0))
    else:
        f = w1.shape[2] // n_e
        w13_spec = pl.BlockSpec((None, d, f), lambda i, e: (layer, 0, e))
        w2_spec = pl.BlockSpec((None, f, d), lambda i, e: (layer, e, 0))
    return pl.pallas_call(
        functools.partial(_ffn_kernel, n_e=n_e, gated=gated),
        grid=(m // tm, n_e),
        in_specs=[pl.BlockSpec((tm, d), lambda i, e: (i, 0)),
                  pl.BlockSpec((tm, LANES), lambda i, e: (i, 0)),
                  w13_spec, w13_spec, w2_spec,
                  pl.BlockSpec((1, d), lambda i, e: (0, 0)),
                  pl.BlockSpec((1, d), lambda i, e: (0, 0))],
        out_specs=pl.BlockSpec((tm, d), lambda i, e: (i, 0)),
        out_shape=jax.ShapeDtypeStruct((m, d), F32),
        scratch_shapes=[pltpu.VMEM((tm, d), BF16), pltpu.VMEM((tm, d), F32)],
        compiler_params=_params("arbitrary", "arbitrary"),
        name="swiglu_residual_ln",
    )(x, gates, w1, w3, w2, g.reshape(1, d), b.reshape(1, d))


def _top_blocks(scores, n_valid, axis):
    blk = lax.broadcasted_iota(jnp.int32, scores.shape, axis)
    s = jnp.where(blk < n_valid, scores, -jnp.inf)
    bias = jnp.full(scores.shape, NEG, F32)
    for _ in range(MOBA_TOPK):
        mx = jnp.max(s, axis=axis, keepdims=True)
        hit = jnp.logical_and(s == mx, mx > -jnp.inf)
        idx = jnp.min(jnp.where(hit, blk, scores.shape[axis]), axis=axis, keepdims=True)
        pick = blk == idx
        bias = jnp.where(pick, 0.0, bias)
        s = jnp.where(pick, -jnp.inf, s)
    return bias


def _moba_prompt_kernel(qt_ref, kt_ref, vt_ref, o_ref, k_ref, vte_ref, means_ref, bias_ref, mblk_ref, s_ref, p_ref,
                        *, nb, group):
    qi = pl.program_id(2)
    blk = MOBA_BLOCK
    tq = qt_ref.shape[1]
    extents = list(range(group, nb, group)) + [nb]

    def for_extent(fn):
        lo = 0
        for nk in extents:
            pl.when(jnp.logical_and(qi >= lo, qi < nk))(functools.partial(fn, nk))
            lo = nk

    @pl.when(qi == 0)
    def _():
        for n in range(nb):
            kn = kt_ref[:, n * blk:(n + 1) * blk].T
            k_ref[n * blk:(n + 1) * blk, :] = kn.astype(BF16)
            means_ref[n:n + 1, :] = jnp.mean(kn, axis=0, keepdims=True)
        for h in range(2):
            vte_ref[h, 0:HEAD_DIM, :] = vt_ref[h * HEAD_DIM:(h + 1) * HEAD_DIM, :].astype(BF16)
            vte_ref[h, HEAD_DIM:, :] = jnp.ones((ONES_ROWS, vt_ref.shape[1]), BF16)
        mblk_ref[...] = jnp.zeros_like(mblk_ref)

    qt = qt_ref[...]
    sub = lax.broadcasted_iota(jnp.int32, (LANES, 1), 0)
    blk_row = lax.broadcasted_iota(jnp.int32, (nb, tq), 0)
    ws, biases = [], []
    for h in range(2):
        head = jnp.logical_and(sub >= h * HEAD_DIM, sub < (h + 1) * HEAD_DIM)
        qth = jnp.where(head, qt, 0.0)
        scores = jnp.dot(means_ref[...], qth, preferred_element_type=F32, precision=HIGHEST)
        biases.append(jnp.where(blk_row == qi, 0.0, _top_blocks(scores, qi, 0)))
        ws.append((qth * (HEAD_DIM ** -0.5 * LOG2E)).astype(BF16))
    w_all = jnp.concatenate(ws, axis=1)
    bias_ref[...] = jnp.concatenate(biases, axis=1)

    def logits(nk):
        s = _dot(k_ref[0:nk * blk, :], w_all)
        s_ref[0:nk * blk, :] = s
        mblk_ref[0:nk, :] = jnp.max(s.reshape(nk, blk, 2 * tq), axis=1)

    for_extent(logits)
    n_used = extents[-1]
    for nk in reversed(extents[:-1]):
        n_used = jnp.where(qi < nk, nk, n_used)

    own = pl.ds(pl.multiple_of(qi * blk, blk), blk)
    key_i = lax.broadcasted_iota(jnp.int32, (blk, 2 * tq), 0)
    q_i = lax.broadcasted_iota(jnp.int32, (blk, 2 * tq), 1) % tq
    s_own = jnp.where(key_i <= q_i, s_ref[own, :], NEG)
    s_ref[own, :] = s_own
    mblk_ref[pl.ds(qi, 1), :] = jnp.max(s_own, axis=0, keepdims=True)
    m = jnp.max(mblk_ref[...] + bias_ref[...], axis=0, keepdims=True)

    def pass_exp(jb, carry):
        rows = pl.ds(pl.multiple_of(jb * blk, blk), blk)
        p_ref[rows, :] = jnp.exp2(s_ref[rows, :] + (bias_ref[pl.ds(jb, 1), :] - m)).astype(BF16)
        return carry

    lax.fori_loop(0, n_used, pass_exp, 0)

    def weighted_values(nk):
        acc = [_dot(vte_ref[h, :, 0:nk * blk], p_ref[0:nk * blk, h * tq:(h + 1) * tq]) for h in range(2)]
        out_t = jnp.concatenate([a[:HEAD_DIM] / a[HEAD_DIM:HEAD_DIM + 1] for a in acc], axis=0)
        o_ref[...] = out_t.T

    for_extent(weighted_values)


def moba_prompt(qt, kt_all, vt_all, layer, m_rows):
    n_batch, _, seq = qt.shape
    nb = seq // MOBA_BLOCK
    tq = MOBA_BLOCK
    n_pairs = ATT_WIDTH // LANES
    kv_spec = pl.BlockSpec((None, None, LANES, seq), lambda b, p, i: (layer, b, p, 0))
    return pl.pallas_call(
        functools.partial(_moba_prompt_kernel, nb=nb, group=4),
        grid=(n_batch, n_pairs, nb),
        in_specs=[pl.BlockSpec((None, LANES, tq), lambda b, p, i: (b, p, i)), kv_spec, kv_spec],
        out_specs=pl.BlockSpec((tq, LANES), lambda b, p, i: (b * nb + i, p)),
        out_shape=jax.ShapeDtypeStruct((m_rows, ATT_WIDTH), F32),
        scratch_shapes=[pltpu.VMEM((seq, LANES), BF16),
                        pltpu.VMEM((2, HEAD_DIM + ONES_ROWS, seq), BF16),
                        pltpu.VMEM((nb, LANES), F32),
                        pltpu.VMEM((nb, 2 * tq), F32),
                        pltpu.VMEM((nb, 2 * tq), F32),
                        pltpu.VMEM((seq, 2 * tq), F32),
                        pltpu.VMEM((seq, 2 * tq), BF16)],
        compiler_params=_params("arbitrary", "arbitrary", "arbitrary"),
        name="moba_prompt",
    )(qt, kt_all, vt_all)


def _moba_sample_kernel(pt_ref, q_ref, ktn_ref, vn_ref, *rest, n_blocks, nbs):
    del pt_ref
    n_pages = 2 * nbs
    k_refs, v_refs = rest[:n_pages], rest[n_pages:2 * n_pages]
    o_ref, qbd_ref, ksum_ref, m_ref, l_ref, oblk_ref = rest[2 * n_pages + 1:]
    step = pl.program_id(1)
    t_new = q_ref.shape[0]
    rows = ATT_HEADS * t_new
    lane = lax.broadcasted_iota(jnp.int32, (1, LANES), 1)

    @pl.when(step == 0)
    def _():
        q = q_ref[...]
        col_head = lax.broadcasted_iota(jnp.int32, (t_new, ATT_WIDTH), 1) // HEAD_DIM
        for h in range(ATT_HEADS):
            qbd_ref[h * t_new:(h + 1) * t_new, :] = jnp.where(col_head == h, q, 0.0)
        ksum_ref[...] = jnp.zeros_like(ksum_ref)
        m_ref[...] = jnp.zeros_like(m_ref)
        l_ref[...] = jnp.zeros_like(l_ref)

    qb = (qbd_ref[...] * (HEAD_DIM ** -0.5)).astype(BF16)
    m_all, l_all, ksum_all = m_ref[...], l_ref[...], ksum_ref[...]
    kt_all = jnp.concatenate([k_ref[...].reshape(ATT_WIDTH, LANES) for k_ref in k_refs], axis=1)
    logits_all = _dot(qb, kt_all.astype(BF16))
    for j in range(nbs):
        n = step * nbs + j
        kt = kt_all[:, j * MOBA_BLOCK:(j + 1) * MOBA_BLOCK]
        vt = jnp.concatenate([v_refs[2 * j][...].reshape(ATT_WIDTH, LANES),
                              v_refs[2 * j + 1][...].reshape(ATT_WIDTH, LANES)], axis=1)
        logits = logits_all[:, j * MOBA_BLOCK:(j + 1) * MOBA_BLOCK]
        mx = jnp.max(logits, axis=1, keepdims=True)
        p = jnp.exp(logits - mx)
        oblk_ref[n] = _dot_nt(p.astype(BF16), vt.astype(BF16))
        m_all = jnp.where(lane == n, mx, m_all)
        l_all = jnp.where(lane == n, jnp.sum(p, axis=1, keepdims=True), l_all)
        ksum_all = jnp.where(lane == n, jnp.sum(kt, axis=1, keepdims=True), ksum_all)
    m_ref[...] = m_all
    l_ref[...] = l_all
    ksum_ref[...] = ksum_all

    @pl.when(step == n_blocks // nbs - 1)
    def _():
        qbd = qbd_ref[...]
        scores = jnp.dot(qbd, ksum_all * (1.0 / MOBA_BLOCK), preferred_element_type=F32,
                         precision=HIGHEST)
        bias = _top_blocks(scores, n_blocks, 1)
        lo = _dot(qb, ktn_ref[...].astype(BF16))
        r_q = lax.broadcasted_iota(jnp.int32, (rows, t_new), 0) % t_new
        c_t = lax.broadcasted_iota(jnp.int32, (rows, t_new), 1)
        lo = jnp.where(c_t <= r_q, lo, NEG)
        m_own = jnp.max(lo, axis=1, keepdims=True)
        p_own = jnp.exp(lo - m_own)
        l_own = jnp.sum(p_own, axis=1, keepdims=True)
        o_own = _dot(p_own.astype(BF16), vn_ref[...].astype(BF16))
        m_sel = m_all + bias
        m_tot = jnp.maximum(jnp.max(m_sel, axis=1, keepdims=True), m_own)
        w = jnp.exp(m_sel - m_tot)
        w_own = jnp.exp(m_own - m_tot)
        l_tot = jnp.sum(w * l_all, axis=1, keepdims=True) + w_own * l_own
        o_tot = w_own * o_own
        for j in range(n_blocks):
            o_tot = o_tot + w[:, j:j + 1] * oblk_ref[j]
        o_tot = o_tot / l_tot
        col_head = lax.broadcasted_iota(jnp.int32, (t_new, ATT_WIDTH), 1) // HEAD_DIM
        out = jnp.zeros((t_new, ATT_WIDTH), F32)
        for h in range(ATT_HEADS):
            out = jnp.where(col_head == h, o_tot[h * t_new:(h + 1) * t_new, :], out)
        o_ref[...] = out


def moba_sample(qkv_s, ktn, cache_kt, cache_vt, page_table, layer, t_new, att, row0):
    n_seq, n_pages = page_table.shape
    assert MOBA_BLOCK == 2 * cache_kt.shape[-1]
    n_blocks = n_pages // 2
    nbs = next(c for c in (8, 4, 1) if n_blocks % c == 0)
    rows = ATT_HEADS * t_new
    page_shape = (None, None) + cache_kt.shape[2:]

    def page_spec(j):
        return pl.BlockSpec(page_shape, lambda b, s, pt: (layer, pt[b, 2 * nbs * s + j], 0, 0, 0))

    pages = [page_spec(j) for j in range(2 * nbs)]
    grid_spec = pltpu.PrefetchScalarGridSpec(
        num_scalar_prefetch=1,
        grid=(n_seq, n_blocks // nbs),
        in_specs=[pl.BlockSpec((t_new, ATT_WIDTH), lambda b, s, pt: (b, 0)),
                  pl.BlockSpec((None, ATT_WIDTH, t_new), lambda b, s, pt: (b, 0, 0)),
                  pl.BlockSpec((t_new, ATT_WIDTH), lambda b, s, pt: (b, 2))] + pages + pages
                 + [pl.BlockSpec(memory_space=pl.ANY)],
        out_specs=pl.BlockSpec((t_new, ATT_WIDTH), lambda b, s, pt: (row0 // t_new + b, 0)),
        scratch_shapes=[pltpu.VMEM((rows, ATT_WIDTH), F32),
                        pltpu.VMEM((ATT_WIDTH, LANES), F32),
                        pltpu.VMEM((rows, LANES), F32), pltpu.VMEM((rows, LANES), F32),
                        pltpu.VMEM((n_blocks, rows, ATT_WIDTH), F32)],
    )
    return pl.pallas_call(
        functools.partial(_moba_sample_kernel, n_blocks=n_blocks, nbs=nbs),
        grid_spec=grid_spec,
        out_shape=jax.ShapeDtypeStruct(att.shape, F32),
        input_output_aliases={4 + 4 * nbs: 0},
        compiler_params=_params("arbitrary", "arbitrary"),
        name="moba_sample",
    )(page_table, qkv_s, ktn, qkv_s, *([cache_kt] * (2 * nbs)), *([cache_vt] * (2 * nbs)), att)


def _shift_rows(x, d, fill):
    row = lax.broadcasted_iota(jnp.int32, x.shape, 0)
    return jnp.where(row >= d, pltpu.roll(x, d, 0), fill)


def _lru_kernel(u_ref, gate_ref, cbuf_ref, h0_ref, cw_ref, cb_ref, wa_ref, ba_ref, wx_ref, bx_ref, lam_ref,
                *rest, tc, n_chunks):
    rec_ref, hl_ref, xbuf, hc = rest[-4:]
    c = pl.program_id(1)

    @pl.when(c == 0)
    def _():
        xbuf[0:SUBLANES, :] = cbuf_ref[...]
        hc[...] = h0_ref[...]

    xbuf[SUBLANES:SUBLANES + tc, :] = u_ref[...]
    w = cw_ref[...]
    uc = cb_ref[...]
    for t in range(CONV_W):
        off = SUBLANES - (CONV_W - 1) + t
        uc = uc + xbuf[off:off + tc, :] * w[t:t + 1, :]
    xbuf[0:SUBLANES, :] = xbuf[tc:tc + SUBLANES, :]

    ub = uc.astype(BF16)
    gate_r = _sigmoid(_dot(ub, wa_ref[...]) + ba_ref[...])
    gate_i = _sigmoid(_dot(ub, wx_ref[...]) + bx_ref[...])
    log_a = -LRU_C * gate_r * _softplus(-lam_ref[...])
    a = jnp.exp(log_a)
    bterm = jnp.sqrt(-jnp.tanh(log_a) * (a * a + 1.0)) * (gate_i * uc)
    row = lax.broadcasted_iota(jnp.int32, a.shape, 0)
    bterm = bterm + jnp.where(row == 0, a * hc[...], 0.0)
    d = 1
    while d < tc:
        b_sh = _shift_rows(bterm, d, 0.0)
        a_sh = _shift_rows(a, d, 1.0)
        bterm = a * b_sh + bterm
        a = a * a_sh
        d *= 2
    h = bterm
    hc[...] = h[tc - 1:tc, :]
    rec_ref[...] = h * _gelu_tanh(gate_ref[...])

    @pl.when(c == n_chunks - 1)
    def _():
        hl_ref[...] = h[tc - 1:tc, :]


def lru_mixer(proj, row0, n_batch, seq, tc, cbuf, h0, cw, cb, wa, ba, wx, bx, lam, rec=None):
    n_chunks = seq // tc
    blk0 = row0 // tc
    carried = [] if rec is None else [rec]
    vec = lambda: pl.BlockSpec((1, LRU_WIDTH), lambda b, c: (0, 0))
    return pl.pallas_call(
        functools.partial(_lru_kernel, tc=tc, n_chunks=n_chunks),
        grid=(n_batch, n_chunks),
        in_specs=[pl.BlockSpec((tc, LRU_WIDTH), lambda b, c: (blk0 + b * n_chunks + c, 0)),
                  pl.BlockSpec((tc, LRU_WIDTH), lambda b, c: (blk0 + b * n_chunks + c, 1)),
                  pl.BlockSpec((None, SUBLANES, LRU_WIDTH), lambda b, c: (b, 0, 0)),
                  pl.BlockSpec((None, 1, LRU_WIDTH), lambda b, c: (b, 0, 0)),
                  pl.BlockSpec((CONV_W, LRU_WIDTH), lambda b, c: (0, 0)),
                  vec(),
                  pl.BlockSpec((LRU_WIDTH, LRU_WIDTH), lambda b, c: (0, 0)),
                  vec(),
                  pl.BlockSpec((LRU_WIDTH, LRU_WIDTH), lambda b, c: (0, 0)),
                  vec(), vec()] + [pl.BlockSpec(memory_space=pl.ANY)] * len(carried),
        out_specs=[pl.BlockSpec((tc, LRU_WIDTH), lambda b, c: (blk0 + b * n_chunks + c, 0)),
                   pl.BlockSpec((None, 1, LRU_WIDTH), lambda b, c: (b, 0, 0))],
        out_shape=[jax.ShapeDtypeStruct((proj.shape[0], LRU_WIDTH), F32),
                   jax.ShapeDtypeStruct((n_batch, 1, LRU_WIDTH), F32)],
        input_output_aliases={11: 0} if carried else {},
        scratch_shapes=[pltpu.VMEM((SUBLANES + tc, LRU_WIDTH), F32), pltpu.VMEM((1, LRU_WIDTH), F32)],
        compiler_params=_params("arbitrary", "arbitrary"),
        name="lru_mixer",
    )(proj, proj, cbuf, h0, cw, cb, wa, ba, wx, bx, lam, *carried)


def _expand_heads(v, g):
    lane = lax.broadcasted_iota(jnp.int32, (1, SSD_GROUP_W), 1)
    out = v[:, SSD_HPG * g + SSD_HPG - 1:SSD_HPG * g + SSD_HPG]
    for k in range(SSD_HPG - 2, -1, -1):
        out = jnp.where(lane < (k + 1) * SSD_HEAD_DIM, v[:, SSD_HPG * g + k:SSD_HPG * g + k + 1], out)
    return out


def _ssd_kernel(x_ref, zx_ref, cbuf_ref, h0_ref, wdt_ref, wdtt_ref, dtb_ref, dtbt_ref, alog_ref, alogt_ref,
                cw_ref, cb_ref, dskip_ref, ng_ref, *rest, tc, n_chunks):
    y_ref, hl_ref, xbuf, st_ref = rest[-4:]
    c = pl.program_id(1)

    @pl.when(c == 0)
    def _():
        xbuf[0:SUBLANES, :] = cbuf_ref[...]
        for g in range(SSD_GROUPS):
            st_ref[g] = h0_ref[g * SSD_GROUP_W:(g + 1) * SSD_GROUP_W, :].T

    xbuf[SUBLANES:SUBLANES + tc, :] = zx_ref[:, SSD_INNER:]
    cw = cw_ref[...]
    cb = cb_ref[...]

    def conv_silu(lo, width):
        acc = cb[:, lo:lo + width]
        for t in range(CONV_W):
            off = SUBLANES - (CONV_W - 1) + t
            acc = acc + xbuf[off:off + tc, lo:lo + width] * cw[t:t + 1, lo:lo + width]
        return _silu(acc)

    xb = x_ref[...].astype(BF16)
    dt = _softplus(_dot(xb, wdt_ref[...]) + dtb_ref[...])
    dtt = _softplus(_dot_nt(wdtt_ref[...], xb) + dtbt_ref[...])
    a_neg = -jnp.exp(alog_ref[...])
    a_negt = -jnp.exp(alogt_ref[...])
    r = lax.broadcasted_iota(jnp.int32, (tc, tc), 0)
    s = lax.broadcasted_iota(jnp.int32, (tc, tc), 1)
    causal = s <= r
    tri = causal.astype(F32)
    cs = jnp.dot(tri, dt * a_neg, preferred_element_type=F32, precision=HIGHEST)
    cst = _dot_nt(dtt * a_negt, tri, precision=HIGHEST)
    cs_end = cs[tc - 1:tc, :]
    w_end = jnp.exp(cs_end - cs) * dt
    ecs = jnp.exp(cs)
    chunk_decay = jnp.exp(cs_end)
    lane = lax.broadcasted_iota(jnp.int32, (1, SSD_GROUP_W), 1)

    for g in range(SSD_GROUPS):
        xg = conv_silu(g * SSD_GROUP_W, SSD_GROUP_W)
        bg = conv_silu(SSD_INNER + g * SSD_STATE, SSD_STATE).astype(BF16)
        cg = conv_silu(SSD_INNER + SSD_GROUPS * SSD_STATE + g * SSD_STATE, SSD_STATE).astype(BF16)
        cbm = _dot_nt(cg, bg)
        y = jnp.zeros((tc, SSD_GROUP_W), F32)
        for k in range(SSD_HPG):
            e = SSD_HPG * g + k
            dec = jnp.exp(jnp.where(causal, cs[:, e:e + 1] - cst[e:e + 1, :], -jnp.inf))
            m = (cbm * dec * dtt[e:e + 1, :]).astype(BF16)
            head = jnp.logical_and(lane >= k * SSD_HEAD_DIM, lane < (k + 1) * SSD_HEAD_DIM)
            y = y + _dot(m, jnp.where(head, xg, 0.0).astype(BF16))
        st = st_ref[g]
        y = y + _dot(cg, st.astype(BF16)) * _expand_heads(ecs, g)
        xw = (xg * _expand_heads(w_end, g)).astype(BF16)
        st_ref[g] = st * _expand_heads(chunk_decay, g) + _dot_tn(bg, xw)
        y = y + dskip_ref[:, g * SSD_GROUP_W:(g + 1) * SSD_GROUP_W] * xg
        y = y * _silu(zx_ref[:, g * SSD_GROUP_W:(g + 1) * SSD_GROUP_W])
        y = y * lax.rsqrt(jnp.mean(y * y, axis=-1, keepdims=True) + RMS_EPS)
        y_ref[:, g * SSD_GROUP_W:(g + 1) * SSD_GROUP_W] = y * ng_ref[:, g * SSD_GROUP_W:(g + 1) * SSD_GROUP_W]

    xbuf[0:SUBLANES, :] = xbuf[tc:tc + SUBLANES, :]

    @pl.when(c == n_chunks - 1)
    def _():
        for g in range(SSD_GROUPS):
            hl_ref[g * SSD_GROUP_W:(g + 1) * SSD_GROUP_W, :] = st_ref[g].T


def ssd_mixer(x, zx, row0, n_batch, seq, tc, cbuf, h0, layer, wdt, wdtt, dtb, alog, cw, cb, dskip, ng, y=None):
    n_chunks = seq // tc
    blk0 = row0 // tc
    m_state = SSD_HEADS * SSD_HEAD_DIM
    carried = [] if y is None else [y]
    const2 = lambda shape: pl.BlockSpec(shape, lambda b, c: (0, 0))
    return pl.pallas_call(
        functools.partial(_ssd_kernel, tc=tc, n_chunks=n_chunks),
        grid=(n_batch, n_chunks),
        in_specs=[pl.BlockSpec((tc, D_MODEL), lambda b, c: (blk0 + b * n_chunks + c, 0)),
                  pl.BlockSpec((tc, SSD_INNER + SSD_CONV_DIM), lambda b, c: (blk0 + b * n_chunks + c, 0)),
                  pl.BlockSpec((None, SUBLANES, SSD_CONV_DIM), lambda b, c: (b, 0, 0)),
                  pl.BlockSpec((None, None, m_state, SSD_STATE), lambda b, c: (layer, b, 0, 0)),
                  const2((D_MODEL, LANES)), const2((LANES, D_MODEL)),
                  const2((1, LANES)), const2((LANES, 1)), const2((1, LANES)), const2((LANES, 1)),
                  const2((CONV_W, SSD_CONV_DIM)), const2((1, SSD_CONV_DIM)),
                  const2((1, SSD_INNER)), const2((1, SSD_INNER))] + [pl.BlockSpec(memory_space=pl.ANY)] * len(carried),
        out_specs=[pl.BlockSpec((tc, SSD_INNER), lambda b, c: (blk0 + b * n_chunks + c, 0)),
                   pl.BlockSpec((None, m_state, SSD_STATE), lambda b, c: (b, 0, 0))],
        out_shape=[jax.ShapeDtypeStruct((x.shape[0], SSD_INNER), F32),
                   jax.ShapeDtypeStruct((n_batch, m_state, SSD_STATE), F32)],
        input_output_aliases={14: 0} if carried else {},
        scratch_shapes=[pltpu.VMEM((SUBLANES + tc, SSD_CONV_DIM), F32),
                        pltpu.VMEM((SSD_GROUPS, SSD_STATE, SSD_GROUP_W), F32)],
        compiler_params=_params("arbitrary", "arbitrary"),
        name="ssd_mixer",
    )(x, zx, cbuf, h0, wdt, wdtt, dtb.reshape(1, LANES), dtb.reshape(LANES, 1),
      alog.reshape(1, LANES), alog.reshape(LANES, 1), cw, cb, dskip, ng, *carried)


def _block_diag(w):
    n, d, e = w.shape
    eye = jnp.eye(n, dtype=w.dtype)
    return (eye[:, None, :, None] * w[:, :, None, :]).reshape(n * d, n * e)


def _conv_tail(buf):
    return jnp.pad(buf, ((0, 0), (SUBLANES - (CONV_W - 1), 0), (0, 0)))


def _last_rows(a, row0, n_batch, seq, col0, width):
    tail = CONV_W - 1
    if seq <= 4 * tail:
        block = lax.slice(a, (row0, col0), (row0 + n_batch * seq, col0 + width))
        return block.reshape(n_batch, seq, width)[:, seq - tail:]
    rows = [lax.slice(a, (row0 + b * seq + seq - tail, col0), (row0 + (b + 1) * seq, col0 + width))
            for b in range(n_batch)]
    return jnp.stack(rows)


def kernel(x_prompt, x_sample, cache_k, cache_v, page_table, state_lru_h, state_lru_conv, state_ssm, state_ssd_conv, w_in_even, lru_conv_w, lru_conv_b, lru_w_a, lru_b_a, lru_w_x, lru_b_x, lru_lambda, w_out_even, ssd_w_in, ssd_conv_w, ssd_conv_b, ssd_dt_bias, ssd_a_log, ssd_d, ssd_norm_g, ssd_w_out, ffn_w1, ffn_w3, ffn_w2, moe_router, moe_w1, moe_w3, moe_w2, ln_mix_g, ln_mix_b, ln_ffn_g, ln_ffn_b):
    bp, sp, d = x_prompt.shape
    bs, ts, _ = x_sample.shape
    mp = bp * sp
    ms = bs * ts
    m = mp + ms
    tm = 640 if m % 640 == 0 else ts * 8
    tm_ffn = 832 if m % 832 == 0 else tm
    n_even, n_odd = w_in_even.shape[0], ssd_w_in.shape[0]
    x = jnp.concatenate([x_prompt.reshape(mp, d), x_sample.reshape(ms, d)], axis=0)

    cache_kt = cache_k.transpose(0, 1, 3, 4, 2)
    cache_vt = cache_v.transpose(0, 1, 3, 4, 2)
    ffn_w = (ffn_w1.astype(BF16), ffn_w3.astype(BF16), ffn_w2.astype(BF16))
    moe_w = (moe_w1.astype(BF16), moe_w3.astype(BF16), moe_w2.astype(BF16))
    m_state = SSD_HEADS * SSD_HEAD_DIM
    ssm_in = state_ssm.reshape(n_odd, bs, m_state, SSD_STATE)

    ks_s, vs_s = [], []
    lruh_p, lruh_s, lrucv_p, lrucv_s = [], [], [], []
    ssm_p, ssm_s, ssdcv_p, ssdcv_s = [], [], [], []
    zeros_lru_cbuf = jnp.zeros((bp, SUBLANES, LRU_WIDTH), F32)
    zeros_lru_h = jnp.zeros((bp, 1, LRU_WIDTH), F32)
    zeros_ssd_cbuf = jnp.zeros((bp, SUBLANES, SSD_CONV_DIM), F32)
    zeros_ssm = jnp.zeros((1, bp, m_state, SSD_STATE), F32)
    no_gates = jnp.zeros((m, LANES), F32)
    kv_all = None

    for layer in range(DEPTH):
        i = layer // 2
        if layer % 2 == 0:
            n_qkv = 3 * ATT_WIDTH
            w_in = w_in_even[i].astype(BF16)
            w_qkv = w_in[:, :n_qkv]
            qt, kt_all, vt_all = qkv_transposed(w_qkv.T, x, bp, sp, 512 if sp % 512 == 0 else sp, i, n_even, kv_all)
            kv_all = (kt_all, vt_all)
            proj = matmul(x, w_in[:, n_qkv:], 2 * LRU_WIDTH, tm, 2 * LRU_WIDTH)
            qkv_s = matmul(x[mp:], w_qkv, n_qkv, ms, n_qkv)
            k_s = qkv_s[:, ATT_WIDTH:2 * ATT_WIDTH]
            v_s = qkv_s[:, 2 * ATT_WIDTH:]
            ktn = k_s.reshape(bs, ts, ATT_WIDTH).transpose(0, 2, 1)
            att = moba_prompt(qt, kt_all, vt_all, i, m)
            att = moba_sample(qkv_s, ktn, cache_kt, cache_vt, page_table, i, ts, att, mp)

            wa = _block_diag(lru_w_a[i]).astype(BF16)
            wx = _block_diag(lru_w_x[i]).astype(BF16)
            lru_args = (lru_conv_w[i], lru_conv_b[i].reshape(1, -1), wa, lru_b_a[i].reshape(1, -1),
                        wx, lru_b_x[i].reshape(1, -1), lru_lambda[i].reshape(1, -1))
            tc_p = 256 if sp % 256 == 0 else sp
            rec, hl_p = lru_mixer(proj, 0, bp, sp, tc_p, zeros_lru_cbuf, zeros_lru_h, *lru_args)
            rec, hl_s = lru_mixer(proj, mp, bs, ts, ts, _conv_tail(state_lru_conv[i]),
                                  state_lru_h[i].reshape(bs, 1, LRU_WIDTH), *lru_args, rec=rec)
            w_out = w_out_even[i].astype(BF16)
            x = matmul_residual_ln([att, rec], [w_out[:ATT_WIDTH], w_out[ATT_WIDTH:]], x,
                                   ln_mix_g[layer], ln_mix_b[layer], tm)
            x = swiglu_residual_ln(x, no_gates, *ffn_w, i, ln_ffn_g[layer], ln_ffn_b[layer], tm_ffn, 2, False)

            ks_s.append(k_s.reshape(bs, ts, ATT_HEADS, HEAD_DIM))
            vs_s.append(v_s.reshape(bs, ts, ATT_HEADS, HEAD_DIM))
            lruh_p.append(hl_p.reshape(bp, LRU_WIDTH))
            lruh_s.append(hl_s.reshape(bs, LRU_WIDTH))
            lrucv_p.append(_last_rows(proj, 0, bp, sp, 0, LRU_WIDTH))
            lrucv_s.append(_last_rows(proj, mp, bs, ts, 0, LRU_WIDTH))
        else:
            w_in = ssd_w_in[i]
            n_zx = SSD_INNER + SSD_CONV_DIM
            zx = matmul(x, w_in[:, :n_zx].astype(BF16), n_zx, tm, 2048)
            wdt = jnp.pad(w_in[:, n_zx:], ((0, 0), (0, LANES - SSD_HEADS))).astype(BF16)
            pad_h = lambda v: jnp.pad(v, (0, LANES - SSD_HEADS))
            dskip = jnp.repeat(ssd_d[i], SSD_HEAD_DIM).reshape(1, SSD_INNER)
            ssd_args = (wdt, wdt.T, pad_h(ssd_dt_bias[i]), pad_h(ssd_a_log[i]), ssd_conv_w[i],
                        ssd_conv_b[i].reshape(1, -1), dskip, ssd_norm_g[i].reshape(1, -1))
            tc_p = math.gcd(sp, SSD_CHUNK)
            y, st_p = ssd_mixer(x, zx, 0, bp, sp, tc_p, zeros_ssd_cbuf, zeros_ssm, 0, *ssd_args)
            y, st_s = ssd_mixer(x, zx, mp, bs, ts, ts, _conv_tail(state_ssd_conv[i]), ssm_in, i, *ssd_args, y=y)
            router_t = jnp.pad(moe_router[i].T, ((0, LANES - N_EXPERTS), (0, 0)))
            x, gates = matmul_residual_ln([y], [ssd_w_out[i].astype(BF16)], x, ln_mix_g[layer], ln_mix_b[layer], tm,
                                          router_t=router_t)
            x = swiglu_residual_ln(x, gates, *moe_w, i, ln_ffn_g[layer], ln_ffn_b[layer], tm_ffn, N_EXPERTS, True)

            ssm_p.append(st_p.reshape(bp, SSD_HEADS, SSD_HEAD_DIM, SSD_STATE))
            ssm_s.append(st_s.reshape(bs, SSD_HEADS, SSD_HEAD_DIM, SSD_STATE))
            ssdcv_p.append(_last_rows(zx, 0, bp, sp, SSD_INNER, SSD_CONV_DIM))
            ssdcv_s.append(_last_rows(zx, mp, bs, ts, SSD_INNER, SSD_CONV_DIM))

    head_major = lambda a: a.reshape(n_even, bp, ATT_HEADS, HEAD_DIM, sp).transpose(0, 1, 4, 2, 3)
    return (x[:mp].reshape(bp, sp, d), x[mp:].reshape(bs, ts, d),
            head_major(kv_all[0]), head_major(kv_all[1]), jnp.stack(ks_s), jnp.stack(vs_s),
            jnp.stack(lruh_p), jnp.stack(lruh_s), jnp.stack(lrucv_p), jnp.stack(lrucv_s),
            jnp.stack(ssm_p), jnp.stack(ssm_s), jnp.stack(ssdcv_p), jnp.stack(ssdcv_s))
```
